```python
import jax, jax.numpy as jnp
from jax import lax
import numpy as np

D_MODEL = 2048
BATCH = 16
SEQ = 256
DEPTH = 2
DEC_BATCH = 4
DEC_SEQ = 1024
PAST_LEN = 512

f32 = jnp.float32
GRID_W = 64
MIX_W = D_MODEL
MLA_W = MIX_W // 2
LRU_W = MIX_W // 4
CONV_W = MIX_W // 4
V_HEAD = 128
MLA_HEADS = MLA_W // V_HEAD
QK_NOPE = 128
QK_ROPE = 64
QK_HEAD = QK_NOPE + QK_ROPE
Q_LORA = D_MODEL // 4
KV_LORA = D_MODEL // 8
ROPE_THETA = 10000.0
Q_BLOCK = 128
LRU_HEADS = 8
LRU_HD = LRU_W // LRU_HEADS
LRU_CONV = 4
RG_C = 8.0
CM_K = 31
PEER_HEADS = 8
N_KEYS = 128
N_EXPERTS = N_KEYS * N_KEYS
PEER_QDIM = 256
PEER_TOPK = 16
TOK_BLOCK = 128
IN_SPLITS = (Q_LORA, KV_LORA, QK_ROPE, LRU_W, LRU_W, 2 * CONV_W)
IN_W = Q_LORA + KV_LORA + QK_ROPE + 2 * LRU_W + 2 * CONV_W
EPS = 1e-6

kernel_name = 'hymba_mla_rglru_conformer_peer_dit_step'


def rms_norm(x, g):
    xf = x.astype(f32)
    y = xf * lax.rsqrt(jnp.mean(xf * xf, -1, keepdims=True) + EPS)
    return (y * g.astype(f32)).astype(x.dtype)


def layer_norm(x, g, b):
    xf = x.astype(f32)
    mu = jnp.mean(xf, -1, keepdims=True)
    var = jnp.mean(jnp.square(xf - mu), -1, keepdims=True)
    return ((xf - mu) * lax.rsqrt(var + 1e-5) * g.astype(f32) + b.astype(f32)).astype(x.dtype)


def depthwise_conv(x, w, b, pad_l, pad_r):
    y = lax.conv_general_dilated(x, w[:, None, :].astype(x.dtype), window_strides=(1,),
                                 padding=[(pad_l, pad_r)], dimension_numbers=('NWC', 'WIO', 'NWC'),
                                 feature_group_count=x.shape[-1])
    return y + b


def axial_rope(n_tokens):
    n_rows = n_tokens // GRID_W
    row = jnp.repeat(jnp.arange(n_rows), GRID_W).astype(f32)
    col = jnp.tile(jnp.arange(GRID_W), n_rows).astype(f32)
    n_freq = QK_ROPE // 4
    inv = 1.0 / (ROPE_THETA ** (jnp.arange(n_freq, dtype=f32) / n_freq))
    ang = jnp.concatenate([row[:, None] * inv, col[:, None] * inv], -1)
    return jnp.cos(ang), jnp.sin(ang)


def rope_tail(x, cos, sin):
    r = x[..., QK_NOPE:].astype(f32)
    r1, r2 = jnp.split(r, 2, -1)
    rot = jnp.concatenate([r1 * cos - r2 * sin, r1 * sin + r2 * cos], -1).astype(x.dtype)
    return jnp.concatenate([x[..., :QK_NOPE], rot], -1)


def mla_kv(ckv, krope, w_kvb, k_g):
    B, T, _ = ckv.shape
    kv = (ckv @ w_kvb).reshape(B, T, MLA_HEADS, QK_NOPE + V_HEAD)
    k_nope, v = kv[..., :QK_NOPE], kv[..., QK_NOPE:]
    k_r = jnp.broadcast_to(krope[:, :, None, :], (B, T, MLA_HEADS, QK_ROPE)).astype(k_nope.dtype)
    k = rms_norm(jnp.concatenate([k_nope, k_r], -1), k_g)
    return k, v


def block_attention(q, k, v):
    B, Tq, H, dk = q.shape
    nb = Tq // Q_BLOCK
    qb = jnp.moveaxis(q.reshape(B, nb, Q_BLOCK, H, dk), 1, 0)
    scale = dk ** -0.5

    def one(q_blk):
        s = jnp.einsum('bqhd,bkhd->bhqk', q_blk, k).astype(f32) * scale
        p = jax.nn.softmax(s, -1).astype(v.dtype)
        return jnp.einsum('bhqk,bkhd->bqhd', p, v)

    o = lax.map(one, qb)
    return jnp.moveaxis(o, 0, 1).reshape(B, Tq, H, v.shape[-1])


def bidir_rglru(xc, w_a, b_a, w_i, b_i, lam, h0):
    B, T, W = xc.shape
    xh = xc.reshape(B, T, LRU_HEADS, LRU_HD)
    r = jax.nn.sigmoid((jnp.einsum('bthi,dhij->dbthj', xh, w_a).reshape(2, B, T, W)
                        + b_a[:, None, None, :]).astype(f32))
    i = jax.nn.sigmoid((jnp.einsum('bthi,dhij->dbthj', xh, w_i).reshape(2, B, T, W)
                        + b_i[:, None, None, :]).astype(f32))
    log_a = -RG_C * r * jax.nn.softplus(-lam.astype(f32))[:, None, None, :]
    a = jnp.exp(log_a)
    u = jnp.sqrt(-jnp.expm1(2.0 * log_a)) * i * xc.astype(f32)[None]
    a = jnp.stack([a[0], a[1][:, ::-1]])
    u = jnp.stack([u[0], u[1][:, ::-1]])

    def step(h, au):
        a_t, u_t = au
        h = a_t * h + u_t
        return h, h

    h_fin, hs = lax.scan(step, h0, (jnp.moveaxis(a, 2, 0), jnp.moveaxis(u, 2, 0)))
    hs = jnp.moveaxis(hs, 0, 2)
    return hs[0] + hs[1][:, ::-1], h_fin


def conformer_conv(zc, dw_w, dw_b, ln_g, ln_b):
    a, b = jnp.split(zc, 2, -1)
    h = a * jax.nn.sigmoid(b)
    h = depthwise_conv(h, dw_w, dw_b, CM_K // 2, CM_K // 2)
    h = layer_norm(h, ln_g, ln_b)
    return jax.nn.silu(h)


def peer(h, w_q, sub_keys, u, v):
    B, T, D = h.shape
    n = B * T
    hf = h.reshape(n, D)
    q = (hf @ w_q).reshape(n, PEER_HEADS, 2, PEER_QDIM // 2)
    s = jnp.einsum('nhpd,hpkd->nhpk', q, sub_keys).astype(f32)
    s1, i1 = lax.top_k(s[:, :, 0], PEER_TOPK)
    s2, i2 = lax.top_k(s[:, :, 1], PEER_TOPK)
    cand_s = (s1[..., :, None] + s2[..., None, :]).reshape(n, PEER_HEADS, PEER_TOPK * PEER_TOPK)
    cand_i = (i1[..., :, None] * N_KEYS + i2[..., None, :]).reshape(n, PEER_HEADS, PEER_TOPK * PEER_TOPK)
    top_s, pos = lax.top_k(cand_s, PEER_TOPK)
    idx = jnp.take_along_axis(cand_i, pos, -1)
    gate = jax.nn.softmax(top_s, -1)
    nblk = n // TOK_BLOCK
    idx = idx.reshape(nblk, TOK_BLOCK, PEER_HEADS * PEER_TOPK)
    gate = gate.reshape(nblk, TOK_BLOCK, PEER_HEADS * PEER_TOPK)
    xb = hf.reshape(nblk, TOK_BLOCK, D)

    def apply(args):
        x_b, i_b, g_b = args
        act = jax.nn.gelu(jnp.einsum('tkd,td->tk', u[i_b], x_b).astype(f32))
        wgt = (g_b * act).astype(x_b.dtype)
        return jnp.einsum('tk,tkd->td', wgt, v[i_b])

    out = lax.map(apply, (xb, idx, gate))
    return out.reshape(B, T, D)


def trunk_layer(x, cvec, p, ctx_ckv=None, ctx_krope=None, h0=None):
    latent = ctx_ckv is not None
    B, T, _ = x.shape
    mod = jax.nn.silu(cvec) @ p['ada_w'] + p['ada_b']
    sh1, sc1, g1, sh2, sc2, g2 = jnp.split(mod[:, None, :], 6, -1)
    hn = rms_norm(x, p['norm_mix_g']) * (1 + sc1) + sh1
    offs = [int(o) for o in np.cumsum(IN_SPLITS)[:-1]]
    zq, zkv, zkr, zx, zg, zc = jnp.split(hn @ p['w_in'], offs, -1)
    q = (rms_norm(zq, p['mla_qa_g']) @ p['mla_w_qb']).reshape(B, T, MLA_HEADS, QK_HEAD)
    q = rms_norm(q, p['mla_q_g'])
    ckv = rms_norm(zkv, p['mla_kva_g'])
    k, v = mla_kv(ckv, zkr, p['mla_w_kvb'], p['mla_k_g'])
    if latent:
        cos, sin = axial_rope(T)
        cos, sin = cos[:, None, :], sin[:, None, :]
        q = rope_tail(q, cos, sin)
        k = rope_tail(k, cos, sin)
        kc, vc = mla_kv(ctx_ckv, ctx_krope, p['mla_w_kvb'], p['mla_k_g'])
        k = jnp.concatenate([kc, k], 1)
        v = jnp.concatenate([vc, v], 1)
    else:
        h0 = jnp.zeros((2, B, LRU_W), f32)
    attn = block_attention(q, k, v).reshape(B, T, MLA_W)
    xc = depthwise_conv(zx, p['lru_conv_w'], p['lru_conv_b'], 2, 1)
    y_lru, h_fin = bidir_rglru(xc, p['lru_w_a'], p['lru_b_a'], p['lru_w_i'], p['lru_b_i'], p['lru_lam'], h0)
    lru = y_lru.astype(x.dtype) * jax.nn.gelu(zg)
    conv = conformer_conv(zc, p['cm_dw_w'], p['cm_dw_b'], p['cm_ln_g'], p['cm_ln_b'])
    g_att, g_lru, g_conv = jnp.split(p['grp_g'], [MLA_W, MLA_W + LRU_W])
    mixed = jnp.concatenate([rms_norm(attn, g_att), rms_norm(lru, g_lru), rms_norm(conv, g_conv)], -1)
    x = x + g1 * (mixed @ p['w_out'])
    hf = rms_norm(x, p['norm_ffn_g']) * (1 + sc2) + sh2
    x = x + g2 * peer(hf, p['peer_w_q'], p['peer_keys'], p['peer_u'], p['peer_v'])
    return x, ckv, zkr, jnp.swapaxes(h_fin, 0, 1).astype(x.dtype)


def setup_inputs(seed: int = 0) -> dict:
    key = jax.random.key(seed)
    ks = iter(jax.random.split(key, 48))

    def nrm(shape, scale=1.0):
        return scale * jax.random.normal(next(ks), shape, f32)

    def gain(shape):
        return 1.0 + nrm(shape, 0.01)

    L = DEPTH
    s_lam = jax.random.uniform(next(ks), (L, 2, LRU_W), f32, 0.9, 0.999) ** (1.0 / RG_C)
    return {
        'x_prompt': nrm((BATCH, SEQ, D_MODEL)),
        'x_sample': nrm((DEC_BATCH, DEC_SEQ, D_MODEL)),
        'cache_ckv': nrm((DEC_BATCH, L, PAST_LEN, KV_LORA)),
        'cache_krope': nrm((DEC_BATCH, L, PAST_LEN, QK_ROPE)),
        'state_lru': nrm((DEC_BATCH, L, 2, LRU_W), 0.5),
        'c': nrm((DEC_BATCH, D_MODEL)),
        'c_ctx': nrm((D_MODEL,)),
        'ada_w': nrm((L, D_MODEL, 6 * D_MODEL), 0.5 * D_MODEL ** -0.5),
        'ada_b': nrm((L, 6 * D_MODEL), 0.01),
        'norm_mix_g': gain((L, D_MODEL)),
        'w_in': nrm((L, D_MODEL, IN_W), D_MODEL ** -0.5),
        'mla_qa_g': gain((L, Q_LORA)),
        'mla_w_qb': nrm((L, Q_LORA, MLA_HEADS * QK_HEAD), Q_LORA ** -0.5),
        'mla_q_g': gain((L, QK_HEAD)),
        'mla_kva_g': gain((L, KV_LORA)),
        'mla_w_kvb': nrm((L, KV_LORA, MLA_HEADS * (QK_NOPE + V_HEAD)), KV_LORA ** -0.5),
        'mla_k_g': gain((L, QK_HEAD)),
        'lru_conv_w': nrm((L, LRU_CONV, LRU_W), LRU_CONV ** -0.5),
        'lru_conv_b': nrm((L, LRU_W), 0.01),
        'lru_w_a': nrm((L, 2, LRU_HEADS, LRU_HD, LRU_HD), LRU_HD ** -0.5),
        'lru_b_a': nrm((L, 2, LRU_W), 0.01),
        'lru_w_i': nrm((L, 2, LRU_HEADS, LRU_HD, LRU_HD), LRU_HD ** -0.5),
        'lru_b_i': nrm((L, 2, LRU_W), 0.01),
        'lru_lam': jnp.log(s_lam) - jnp.log1p(-s_lam),
        'cm_dw_w': nrm((L, CM_K, CONV_W), CM_K ** -0.5),
        'cm_dw_b': nrm((L, CONV_W), 0.01),
        'cm_ln_g': gain((L, CONV_W)),
        'cm_ln_b': nrm((L, CONV_W), 0.01),
        'grp_g': gain((L, MIX_W)),
        'w_out': nrm((L, MIX_W, D_MODEL), MIX_W ** -0.5),
        'norm_ffn_g': gain((L, D_MODEL)),
        'peer_w_q': nrm((L, D_MODEL, PEER_HEADS * PEER_QDIM), D_MODEL ** -0.5),
        'peer_keys': nrm((L, PEER_HEADS, 2, N_KEYS, PEER_QDIM // 2), (PEER_QDIM // 2) ** -0.5),
        'peer_u': nrm((L, N_EXPERTS, D_MODEL), D_MODEL ** -0.5),
        'peer_v': nrm((L, N_EXPERTS, D_MODEL), 0.25),
    }


def reference(x_prompt, x_sample, cache_ckv, cache_krope, state_lru, c, c_ctx,
              ada_w, ada_b, norm_mix_g, w_in, mla_qa_g, mla_w_qb, mla_q_g, mla_kva_g, mla_w_kvb, mla_k_g,
              lru_conv_w, lru_conv_b, lru_w_a, lru_b_a, lru_w_i, lru_b_i, lru_lam,
              cm_dw_w, cm_dw_b, cm_ln_g, cm_ln_b, grp_g, w_out, norm_ffn_g,
              peer_w_q, peer_keys, peer_u, peer_v):
    stacked = {
        'ada_w': ada_w, 'ada_b': ada_b, 'norm_mix_g': norm_mix_g, 'w_in': w_in,
        'mla_qa_g': mla_qa_g, 'mla_w_qb': mla_w_qb, 'mla_q_g': mla_q_g, 'mla_kva_g': mla_kva_g,
        'mla_w_kvb': mla_w_kvb, 'mla_k_g': mla_k_g,
        'lru_conv_w': lru_conv_w, 'lru_conv_b': lru_conv_b, 'lru_w_a': lru_w_a, 'lru_b_a': lru_b_a,
        'lru_w_i': lru_w_i, 'lru_b_i': lru_b_i, 'lru_lam': lru_lam,
        'cm_dw_w': cm_dw_w, 'cm_dw_b': cm_dw_b, 'cm_ln_g': cm_ln_g, 'cm_ln_b': cm_ln_b,
        'grp_g': grp_g, 'w_out': w_out, 'norm_ffn_g': norm_ffn_g,
        'peer_w_q': peer_w_q, 'peer_keys': peer_keys, 'peer_u': peer_u, 'peer_v': peer_v,
    }
    yp, ys = x_prompt, x_sample
    ckv_list, kr_list, h_list = [], [], []
    for l in range(DEPTH):
        p = {name: arr[l] for name, arr in stacked.items()}
        yp, ckv_l, kr_l, h_l = trunk_layer(yp, c_ctx[None, :], p)
        ckv_list.append(ckv_l)
        kr_list.append(kr_l)
        h_list.append(h_l)
        h0 = jnp.swapaxes(state_lru[:, l], 0, 1).astype(f32)
        ys, _, _, _ = trunk_layer(ys, c, p, cache_ckv[:, l], cache_krope[:, l], h0)
    new_ckv = jnp.stack(ckv_list, 1)
    new_krope = jnp.stack(kr_list, 1)
    new_lru = jnp.stack(h_list, 1)
    return (yp, ys, new_ckv, new_krope, new_lru)
```

```python
import functools

import jax
import jax.numpy as jnp
from jax import lax
from jax.experimental import pallas as pl
from jax.experimental.pallas import tpu as pltpu

f32 = jnp.float32
bf16 = jnp.bfloat16
i32 = jnp.int32

D = 2048
NB_P, T_P = 16, 256
NB_S, T_S = 4, 1024
T_CTX = 512
NP = NB_P * T_P
NS = NB_S * T_S
NT = NP + NS
T_KV = T_CTX + T_S
NKV = NB_S * T_KV + NP
L = 2
HEADS = 8
QK_NOPE, QK_ROPE, V_HEAD = 128, 64, 128
QK_HEAD = QK_NOPE + QK_ROPE
Q_LORA, KV_LORA = 512, 256
LRU_W, CONV_W = 512, 512
LRU_HEADS, LRU_HD = 8, 64
LRU_K, CM_K = 4, 31
RG_C = 8.0
PEER_HEADS, N_KEYS, TOPK = 8, 128, 16
N_EXPERTS = N_KEYS * N_KEYS
EPS = 1e-6
LN_EPS = 1e-5
ROPE_THETA = 10000.0
GRID_W = 64

LANES = 128
HEAD_PAD = 256
IN_W_PAD = 2944
TM = 512
TQ_S = 512
TL = 128
PEER_TM = 512
PEER_TE = 1024
ROWS_PER_TE = PEER_TE // N_KEYS
VMEM_LIMIT = 56 * 1024 * 1024


def _cparams(sem):
    return pltpu.CompilerParams(dimension_semantics=sem, vmem_limit_bytes=VMEM_LIMIT)


def _mod_row(i, tm):
    return jnp.where(i < NP // tm, 0, 1 + (i * tm - NP) // T_S)


def _rms(x, g):
    return (x * lax.rsqrt(jnp.mean(x * x, -1, keepdims=True) + EPS)) * g


def _gelu_tanh(x):
    return 0.5 * x * (1.0 + jnp.tanh(0.7978845608028654 * (x + 0.044715 * (x * x * x))))


def _sigmoid(x):
    return 1.0 / (1.0 + jnp.exp(-x))


def _rope(a, cos, sin):
    lane = lax.broadcasted_iota(i32, a.shape, 1)
    half = QK_ROPE // 2
    sw = jnp.where(lane < half, pltpu.roll(a, LANES - half, 1), pltpu.roll(a, half, 1))
    return a * cos + sw * sin


def _mod_kernel(c_ref, w_ref, b_ref, o_ref):
    c = c_ref[...]
    a = (c * _sigmoid(c)).astype(bf16)
    o_ref[...] = jnp.dot(a, w_ref[...].astype(bf16), preferred_element_type=f32) + b_ref[...]


def _modulation(cs, ada_w, ada_b):
    tn = 1536
    return pl.pallas_call(
        _mod_kernel,
        out_shape=jax.ShapeDtypeStruct((L, 8, 6 * D), f32),
        grid=(L, 6 * D // tn),
        in_specs=[pl.BlockSpec((8, D), lambda l, j: (0, 0)),
                  pl.BlockSpec((None, D, tn), lambda l, j: (l, 0, j)),
                  pl.BlockSpec((None, 1, tn), lambda l, j: (l, 0, j))],
        out_specs=pl.BlockSpec((None, 8, tn), lambda l, j: (l, 0, j)),
        compiler_params=_cparams(("arbitrary", "arbitrary")),
        name="ada_mod",
    )(cs, ada_w, ada_b.reshape(L, 1, 6 * D))


def _inproj_kernel(has_res, *refs):
    if has_res:
        (x_ref, p_ref, mprev_ref, mod_ref, ng_ref, w_ref, qag_ref, kvag_ref,
         xo_ref, qn_ref, ckv_ref, zkr_ref, zx_ref, zg_ref, zc_ref) = refs
        x = x_ref[...] + mprev_ref[5:6, :] * p_ref[...]
        xo_ref[...] = x
    else:
        (x_ref, mod_ref, ng_ref, w_ref, qag_ref, kvag_ref,
         qn_ref, ckv_ref, zkr_ref, zx_ref, zg_ref, zc_ref) = refs
        x = x_ref[...]
    m = mod_ref[...]
    hn = _rms(x, ng_ref[...]) * (1.0 + m[1:2, :]) + m[0:1, :]
    hb = hn.astype(bf16)

    def seg(a, b):
        return jnp.dot(hb, w_ref[:, a:b], preferred_element_type=f32)

    qn_ref[...] = _rms(seg(0, 512), qag_ref[...]).astype(bf16)
    ckv_ref[...] = _rms(seg(512, 768), kvag_ref[...])
    zkr_ref[...] = seg(768, 896)
    zx_ref[...] = seg(896, 1408)
    zg_ref[...] = seg(1408, 1920)
    zc_ref[...] = seg(1920, 2944)


def _inproj(x, peer_out, mod_prev, mod_l, ng, w_in_p, qag, kvag):
    has_res = peer_out is not None
    row = lambda i: (i, 0)
    modspec = pl.BlockSpec((None, 6, D), lambda i: (_mod_row(i, TM), 0, 0))
    full = lambda shape: pl.BlockSpec(shape, lambda i: (0,) * len(shape))
    in_specs = [pl.BlockSpec((TM, D), row)]
    args = [x]
    if has_res:
        in_specs += [pl.BlockSpec((TM, D), row), modspec]
        args += [peer_out, mod_prev]
    in_specs += [modspec, full((1, D)), full((D, IN_W_PAD)), full((1, Q_LORA)), full((1, KV_LORA))]
    args += [mod_l, ng, w_in_p, qag, kvag]
    out_shape = [jax.ShapeDtypeStruct((NT, Q_LORA), bf16),
                 jax.ShapeDtypeStruct((NT, KV_LORA), f32),
                 jax.ShapeDtypeStruct((NT, LANES), f32),
                 jax.ShapeDtypeStruct((NT, LRU_W), f32),
                 jax.ShapeDtypeStruct((NT, LRU_W), f32),
                 jax.ShapeDtypeStruct((NT, 2 * CONV_W), f32)]
    out_specs = [pl.BlockSpec((TM, Q_LORA), row), pl.BlockSpec((TM, KV_LORA), row),
                 pl.BlockSpec((TM, LANES), row), pl.BlockSpec((TM, LRU_W), row),
                 pl.BlockSpec((TM, LRU_W), row), pl.BlockSpec((TM, 2 * CONV_W), row)]
    if has_res:
        out_shape = [jax.ShapeDtypeStruct((NT, D), f32)] + out_shape
        out_specs = [pl.BlockSpec((TM, D), row)] + out_specs
    outs = pl.pallas_call(
        functools.partial(_inproj_kernel, has_res),
        out_shape=out_shape, grid=(NT // TM,), in_specs=in_specs, out_specs=out_specs,
        compiler_params=_cparams(("arbitrary",)), name="in_proj",
    )(*args)
    if has_res:
        return outs[0], outs[1:]
    return x, outs


def _qproj_kernel(qn_ref, w_ref, g_ref, cos_ref, sin_ref, q_ref):
    qn = qn_ref[...]
    cos, sin = cos_ref[...], sin_ref[...]
    scale = QK_HEAD ** -0.5
    for h in range(HEADS):
        lo = h * HEAD_PAD
        qh = jnp.dot(qn, w_ref[:, lo:lo + HEAD_PAD], preferred_element_type=f32)
        r = lax.rsqrt(jnp.sum(qh * qh, -1, keepdims=True) * (1.0 / QK_HEAD) + EPS)
        qh = (qh * r) * g_ref[:, lo:lo + HEAD_PAD]
        q_ref[:, lo:lo + LANES] = (qh[:, :LANES] * scale).astype(bf16)
        q_ref[:, lo + LANES:lo + HEAD_PAD] = (_rope(qh[:, LANES:], cos, sin) * scale).astype(bf16)


def _qproj(qn, w_qb_p, qg_p, cos_t, sin_t):
    nblk_p = NP // TM
    tab = lambda i: (jnp.where(i < nblk_p, T_S // TM, (i - nblk_p) % (T_S // TM)), 0)
    return pl.pallas_call(
        _qproj_kernel,
        out_shape=jax.ShapeDtypeStruct((NT, HEADS * HEAD_PAD), bf16),
        grid=(NT // TM,),
        in_specs=[pl.BlockSpec((TM, Q_LORA), lambda i: (i, 0)),
                  pl.BlockSpec((Q_LORA, HEADS * HEAD_PAD), lambda i: (0, 0)),
                  pl.BlockSpec((1, HEADS * HEAD_PAD), lambda i: (0, 0)),
                  pl.BlockSpec((TM, LANES), tab), pl.BlockSpec((TM, LANES), tab)],
        out_specs=pl.BlockSpec((TM, HEADS * HEAD_PAD), lambda i: (i, 0)),
        compiler_params=_cparams(("arbitrary",)), name="q_proj",
    )(qn, w_qb_p, qg_p, cos_t, sin_t)


def _kvproj_kernel(ckv_ref, kr_ref, w_ref, gn_ref, gr_ref, cos_ref, sin_ref, k_ref, v_ref):
    ckv = ckv_ref[...].astype(bf16)
    kr = kr_ref[...]
    ssr = jnp.sum(kr * kr, -1, keepdims=True)
    cos, sin = cos_ref[...], sin_ref[...]
    for h in range(HEADS):
        lo = h * HEAD_PAD
        kv = jnp.dot(ckv, w_ref[:, lo:lo + HEAD_PAD], preferred_element_type=f32)
        kn = kv[:, :QK_NOPE]
        r = lax.rsqrt((jnp.sum(kn * kn, -1, keepdims=True) + ssr) * (1.0 / QK_HEAD) + EPS)
        k_ref[:, lo:lo + LANES] = ((kn * r) * gn_ref[...]).astype(bf16)
        k_ref[:, lo + LANES:lo + HEAD_PAD] = _rope((kr * r) * gr_ref[...], cos, sin).astype(bf16)
        v_ref[:, h * V_HEAD:(h + 1) * V_HEAD] = kv[:, QK_NOPE:].astype(bf16)


def _kvproj(ckv_all, kr_all, w_kvb, kgn, kgr, cos_t, sin_t):
    nblk_s = NB_S * T_KV // TM
    per = T_KV // TM
    ident = T_S // TM
    tab = lambda i: (jnp.where(i < nblk_s, (i % per + ident) % per, ident), 0)
    return pl.pallas_call(
        _kvproj_kernel,
        out_shape=[jax.ShapeDtypeStruct((NKV, HEADS * HEAD_PAD), bf16),
                   jax.ShapeDtypeStruct((NKV, HEADS * V_HEAD), bf16)],
        grid=(NKV // TM,),
        in_specs=[pl.BlockSpec((TM, KV_LORA), lambda i: (i, 0)),
                  pl.BlockSpec((TM, LANES), lambda i: (i, 0)),
                  pl.BlockSpec((KV_LORA, HEADS * HEAD_PAD), lambda i: (0, 0)),
                  pl.BlockSpec((1, LANES), lambda i: (0, 0)),
                  pl.BlockSpec((1, LANES), lambda i: (0, 0)),
                  pl.BlockSpec((TM, LANES), tab), pl.BlockSpec((TM, LANES), tab)],
        out_specs=[pl.BlockSpec((TM, HEADS * HEAD_PAD), lambda i: (i, 0)),
                   pl.BlockSpec((TM, HEADS * V_HEAD), lambda i: (i, 0))],
        compiler_params=_cparams(("arbitrary",)), name="kv_proj",
    )(ckv_all, kr_all, w_kvb, kgn, kgr, cos_t, sin_t)


def _attn_kernel(q_ref, k_ref, v_ref, o_ref):
    s = lax.dot_general(q_ref[...], k_ref[...], (((1,), (1,)), ((), ())), preferred_element_type=f32)
    p = jnp.exp(s - jnp.max(s, -1, keepdims=True))
    o = jnp.dot(p.astype(bf16), v_ref[...], preferred_element_type=f32)
    o_ref[...] = o / jnp.sum(p, -1, keepdims=True)


def _attention(q, k, v, nb, tq_total, tq, tk, q_blk0, kv_blk0, out_rows):
    nq = tq_total // tq
    return pl.pallas_call(
        _attn_kernel,
        out_shape=jax.ShapeDtypeStruct((out_rows, HEADS * V_HEAD), f32),
        grid=(nb, HEADS, nq),
        in_specs=[pl.BlockSpec((tq, HEAD_PAD), lambda b, h, i: (q_blk0 + b * nq + i, h)),
                  pl.BlockSpec((tk, HEAD_PAD), lambda b, h, i: (kv_blk0 + b, h)),
                  pl.BlockSpec((tk, V_HEAD), lambda b, h, i: (kv_blk0 + b, h))],
        out_specs=pl.BlockSpec((tq, V_HEAD), lambda b, h, i: (b * nq + i, h)),
        compiler_params=_cparams(("arbitrary", "arbitrary", "arbitrary")), name="attention",
    )(q, k, v)


def _seqmix_kernel(T, zx_ref, zg_ref, zc_ref, h0_ref, lcw_ref, lcb_ref, wg_ref, bg_ref, lam_ref,
                   glru_ref, cw_ref, cb_ref, lng_ref, lnb_ref, gconv_ref,
                   lru_ref, conv_ref, hfin_ref, xpad, a_s, u_s, hs, cpad):
    x = zx_ref[...]
    xpad[0:8, :] = jnp.zeros((8, LRU_W), f32)
    xpad[8:8 + T, :] = x
    xpad[8 + T:16 + T, :] = jnp.zeros((8, LRU_W), f32)
    xc = lcb_ref[...] + lcw_ref[0:1, :] * xpad[6:6 + T, :]
    for k in range(1, LRU_K):
        xc = xc + lcw_ref[k:k + 1, :] * xpad[6 + k:6 + k + T, :]
    g = jnp.dot(xc.astype(bf16), wg_ref[...], preferred_element_type=f32) + bg_ref[...]
    lam = lam_ref[...]
    sp = jnp.maximum(-lam, 0.0) + jnp.log(1.0 + jnp.exp(-jnp.abs(lam)))
    for d in range(2):
        r = _sigmoid(g[:, d * LRU_W:(d + 1) * LRU_W])
        gi = _sigmoid(g[:, (2 + d) * LRU_W:(3 + d) * LRU_W])
        log_a = (-RG_C * r) * sp[d:d + 1, :]
        a = jnp.exp(log_a)
        a_s[d] = a
        u_s[d] = (jnp.sqrt(1.0 - jnp.exp(2.0 * log_a)) * gi) * xc

    h0 = h0_ref[...]

    def blk(i, carry):
        hf, hb = carry
        r0 = pl.multiple_of(i * 8, 8)
        af = a_s[0, pl.ds(r0, 8), :]
        uf = u_s[0, pl.ds(r0, 8), :]
        rows = []
        for j in range(8):
            hf = af[j:j + 1, :] * hf + uf[j:j + 1, :]
            rows.append(hf)
        hs[0, pl.ds(r0, 8), :] = jnp.concatenate(rows, 0)
        r1 = pl.multiple_of(T - 8 - i * 8, 8)
        ab = a_s[1, pl.ds(r1, 8), :]
        ub = u_s[1, pl.ds(r1, 8), :]
        rows = [None] * 8
        for j in range(7, -1, -1):
            hb = ab[j:j + 1, :] * hb + ub[j:j + 1, :]
            rows[j] = hb
        hs[1, pl.ds(r1, 8), :] = jnp.concatenate(rows, 0)
        return hf, hb

    hf, hb = lax.fori_loop(0, T // 8, blk, (h0[0:1, :], h0[1:2, :]))
    hfin_ref[0:1, :] = hf
    hfin_ref[1:2, :] = hb
    y = (hs[0] + hs[1]) * _gelu_tanh(zg_ref[...])
    lru_ref[...] = _rms(y, glru_ref[...]).astype(bf16)

    hglu = zc_ref[:, 0:CONV_W] * _sigmoid(zc_ref[:, CONV_W:2 * CONV_W])
    cpad[0:16, :] = jnp.zeros((16, CONV_W), f32)
    cpad[16:16 + T, :] = hglu
    cpad[16 + T:32 + T, :] = jnp.zeros((16, CONV_W), f32)
    acc = cb_ref[...] + cw_ref[0:1, :] * cpad[1:1 + T, :]
    for k in range(1, CM_K):
        acc = acc + cw_ref[k:k + 1, :] * cpad[1 + k:1 + k + T, :]
    mu = jnp.mean(acc, -1, keepdims=True)
    cen = acc - mu
    var = jnp.mean(cen * cen, -1, keepdims=True)
    hln = (cen * lax.rsqrt(var + LN_EPS)) * lng_ref[...] + lnb_ref[...]
    hsl = hln * _sigmoid(hln)
    conv_ref[...] = _rms(hsl, gconv_ref[...]).astype(bf16)


def _seqmix(zx, zg, zc, h0, nb, T, blk0, wts):
    seq = lambda w: pl.BlockSpec((T, w), lambda b: (blk0 + b, 0))
    full = lambda a: pl.BlockSpec(a.shape, lambda b: (0,) * a.ndim)
    return pl.pallas_call(
        functools.partial(_seqmix_kernel, T),
        out_shape=[jax.ShapeDtypeStruct((nb * T, LRU_W), bf16),
                   jax.ShapeDtypeStruct((nb * T, CONV_W), bf16),
                   jax.ShapeDtypeStruct((nb, 2, LRU_W), f32)],
        grid=(nb,),
        in_specs=[seq(LRU_W), seq(LRU_W), seq(2 * CONV_W),
                  pl.BlockSpec((None, 2, LRU_W), lambda b: (b, 0, 0))] + [full(a) for a in wts],
        out_specs=[pl.BlockSpec((T, LRU_W), lambda b: (b, 0)),
                   pl.BlockSpec((T, CONV_W), lambda b: (b, 0)),
                   pl.BlockSpec((None, 2, LRU_W), lambda b: (b, 0, 0))],
        scratch_shapes=[pltpu.VMEM((T + 16, LRU_W), f32), pltpu.VMEM((2, T, LRU_W), f32),
                        pltpu.VMEM((2, T, LRU_W), f32), pltpu.VMEM((2, T, LRU_W), f32),
                        pltpu.VMEM((T + 32, CONV_W), f32)],
        compiler_params=_cparams(("arbitrary",)), name="seq_mix",
    )(zx, zg, zc, h0, *wts)


def _outproj_kernel(attn_ref, lru_ref, conv_ref, x_ref, mod_ref, gatt_ref, wo_ref, nf_ref,
                    x1_ref, hf_ref):
    an = _rms(attn_ref[...], gatt_ref[...]).astype(bf16)
    mixed = jnp.dot(an, wo_ref[0:1024, :], preferred_element_type=f32)
    mixed = mixed + jnp.dot(lru_ref[...], wo_ref[1024:1536, :], preferred_element_type=f32)
    mixed = mixed + jnp.dot(conv_ref[...], wo_ref[1536:2048, :], preferred_element_type=f32)
    m = mod_ref[...]
    x1 = x_ref[...] + m[2:3, :] * mixed
    x1_ref[...] = x1
    hf_ref[...] = (_rms(x1, nf_ref[...]) * (1.0 + m[4:5, :]) + m[3:4, :]).astype(bf16)


def _outproj(attn, lru_n, conv_n, x, mod_l, gatt, w_out, nf):
    row = lambda i: (i, 0)
    return pl.pallas_call(
        _outproj_kernel,
        out_shape=[jax.ShapeDtypeStruct((NT, D), f32), jax.ShapeDtypeStruct((NT, D), bf16)],
        grid=(NT // TM,),
        in_specs=[pl.BlockSpec((TM, HEADS * V_HEAD), row), pl.BlockSpec((TM, LRU_W), row),
                  pl.BlockSpec((TM, CONV_W), row), pl.BlockSpec((TM, D), row),
                  pl.BlockSpec((None, 6, D), lambda i: (_mod_row(i, TM), 0, 0)),
                  pl.BlockSpec((1, HEADS * V_HEAD), lambda i: (0, 0)),
                  pl.BlockSpec((D, D), lambda i: (0, 0)),
                  pl.BlockSpec((1, D), lambda i: (0, 0))],
        out_specs=[pl.BlockSpec((TM, D), row), pl.BlockSpec((TM, D), row)],
        compiler_params=_cparams(("arbitrary",)), name="out_proj",
    )(attn, lru_n, conv_n, x, mod_l, gatt, w_out, nf)


def _peerq_kernel(hf_ref, wq_ref, keys_ref, st_ref):
    q = jnp.dot(hf_ref[...], wq_ref[...], preferred_element_type=f32).astype(bf16)
    for hp in range(2 * PEER_HEADS):
        st_ref[hp] = lax.dot_general(keys_ref[hp], q[:, hp * N_KEYS:(hp + 1) * N_KEYS],
                                     (((1,), (1,)), ((), ())), preferred_element_type=f32)


def _peerq(hf, wq, keys):
    return pl.pallas_call(
        _peerq_kernel,
        out_shape=jax.ShapeDtypeStruct((2 * PEER_HEADS, N_KEYS, NT), f32),
        grid=(NT // TM,),
        in_specs=[pl.BlockSpec((TM, D), lambda i: (i, 0)),
                  pl.BlockSpec((D, D), lambda i: (0, 0)),
                  pl.BlockSpec((2 * PEER_HEADS, N_KEYS, N_KEYS), lambda i: (0, 0, 0))],
        out_specs=pl.BlockSpec((2 * PEER_HEADS, N_KEYS, TM), lambda i: (0, 0, i)),
        compiler_params=_cparams(("arbitrary",)), name="peer_query",
    )(hf, wq, keys)


def _top16(x):
    kio = lax.broadcasted_iota(i32, x.shape, 0).astype(f32)
    rio = lax.broadcasted_iota(i32, (TOPK, x.shape[1]), 0)

    def body(r, c):
        x, rank, tv = c
        m = jnp.max(x, axis=0, keepdims=True)
        idx = jnp.min(jnp.where(x == m, kio, float(N_KEYS)), axis=0, keepdims=True)
        hit = kio == idx
        return (jnp.where(hit, -jnp.inf, x), jnp.where(hit, r.astype(f32), rank), jnp.where(rio == r, m, tv))

    init = (x, jnp.full(x.shape, float(TOPK), f32), jnp.zeros((TOPK, x.shape[1]), f32))
    _, rank, tv = lax.fori_loop(0, TOPK, body, init)
    return rank, tv


def _topk_kernel(st_ref, rank2_ref, w2_ref, cnt1_ref, w1_ref):
    def head(h, _):
        x1 = st_ref[2 * h]
        x2 = st_ref[2 * h + 1]
        rank1, tv1 = _top16(x1)
        rank2, tv2 = _top16(x2)
        cand = jnp.concatenate([tv1[r:r + 1, :] + tv2 for r in range(TOPK)], axis=0)
        pio = lax.broadcasted_iota(i32, cand.shape, 0).astype(f32)

        def body(_, c):
            cand, sel = c
            m = jnp.max(cand, axis=0, keepdims=True)
            pos = jnp.min(jnp.where(cand == m, pio, float(TOPK * TOPK)), axis=0, keepdims=True)
            hit = pio == pos
            return jnp.where(hit, -jnp.inf, cand), jnp.where(hit, 1.0, sel)

        _, sel = lax.fori_loop(0, TOPK, body, (cand, jnp.zeros(cand.shape, f32)))
        e1 = jnp.exp(tv1 - tv1[0:1, :])
        e2 = jnp.exp(tv2 - tv2[0:1, :])
        z = jnp.zeros((1, x1.shape[1]), f32)
        cnt1 = jnp.zeros(x1.shape, f32)
        for r in range(TOPK):
            blk = sel[r * TOPK:(r + 1) * TOPK, :]
            z = z + e1[r:r + 1, :] * jnp.sum(blk * e2, axis=0, keepdims=True)
            cnt1 = jnp.where(rank1 == r, jnp.sum(blk, axis=0, keepdims=True), cnt1)
        rank2_ref[h] = rank2
        w2_ref[h] = jnp.exp(x2 - tv2[0:1, :])
        cnt1_ref[h] = cnt1
        w1_ref[h] = jnp.exp(x1 - tv1[0:1, :]) / z
        return 0

    lax.fori_loop(0, PEER_HEADS, head, 0)


def _topk(st):
    spec = pl.BlockSpec((PEER_HEADS, N_KEYS, TL), lambda i: (0, 0, i))
    shp = jax.ShapeDtypeStruct((PEER_HEADS, N_KEYS, NT), f32)
    return pl.pallas_call(
        _topk_kernel,
        out_shape=[shp, shp, shp, shp],
        grid=(NT // TL,),
        in_specs=[pl.BlockSpec((2 * PEER_HEADS, N_KEYS, TL), lambda i: (0, 0, i))],
        out_specs=[spec, spec, spec, spec],
        compiler_params=_cparams(("arbitrary",)), name="peer_topk",
    )(st)


def _peer_kernel(hf_ref, u_ref, vt_ref, rank2_ref, w2_ref, cnt1_ref, w1_ref, o_ref, acc, at):
    j = pl.program_id(1)

    @pl.when(j == 0)
    def _():
        acc[...] = jnp.zeros(acc.shape, f32)

    st = lax.dot_general(u_ref[...], hf_ref[...], (((1,), (1,)), ((), ())), preferred_element_type=f32)
    for e in range(ROWS_PER_TE):
        g = jnp.zeros((N_KEYS, PEER_TM), f32)
        for h in range(PEER_HEADS):
            c = cnt1_ref[h, e:e + 1, :]
            w = w1_ref[h, e:e + 1, :]
            g = g + jnp.where(rank2_ref[h] < c, w2_ref[h] * w, 0.0)
        s = st[e * N_KEYS:(e + 1) * N_KEYS, :]
        at[e * N_KEYS:(e + 1) * N_KEYS, :] = (_gelu_tanh(s) * g).astype(bf16)
    acc[...] += jnp.dot(vt_ref[...], at[...], preferred_element_type=f32)

    @pl.when(j == pl.num_programs(1) - 1)
    def _():
        o_ref[...] = acc[...].T


def _peer(hf, u, vt, rank2, w2, cnt1, w1):
    big = pl.BlockSpec((PEER_HEADS, N_KEYS, PEER_TM), lambda i, j: (0, 0, i))
    small = pl.BlockSpec((PEER_HEADS, ROWS_PER_TE, PEER_TM), lambda i, j: (0, j, i))
    return pl.pallas_call(
        _peer_kernel,
        out_shape=jax.ShapeDtypeStruct((NT, D), f32),
        grid=(NT // PEER_TM, N_EXPERTS // PEER_TE),
        in_specs=[pl.BlockSpec((PEER_TM, D), lambda i, j: (i, 0)),
                  pl.BlockSpec((PEER_TE, D), lambda i, j: (j, 0)),
                  pl.BlockSpec((D, PEER_TE), lambda i, j: (0, j)),
                  big, big, small, small],
        out_specs=pl.BlockSpec((PEER_TM, D), lambda i, j: (i, 0)),
        scratch_shapes=[pltpu.VMEM((D, PEER_TM), f32), pltpu.VMEM((PEER_TE, PEER_TM), bf16)],
        compiler_params=_cparams(("arbitrary", "arbitrary")), name="peer_experts",
    )(hf, u, vt, rank2, w2, cnt1, w1)


def _residual_kernel(x_ref, p_ref, mod_ref, o_ref):
    o_ref[...] = x_ref[...] + mod_ref[5:6, :] * p_ref[...]


def _residual(x1, peer_out, mod_l):
    row = lambda i: (i, 0)
    return pl.pallas_call(
        _residual_kernel,
        out_shape=jax.ShapeDtypeStruct((NT, D), f32),
        grid=(NT // TM,),
        in_specs=[pl.BlockSpec((TM, D), row), pl.BlockSpec((TM, D), row),
                  pl.BlockSpec((None, 6, D), lambda i: (_mod_row(i, TM), 0, 0))],
        out_specs=pl.BlockSpec((TM, D), row),
        compiler_params=_cparams(("arbitrary",)), name="residual",
    )(x1, peer_out, mod_l)


def _rope_tables():
    pos = jnp.arange(T_S)
    row = (pos // GRID_W).astype(f32)
    col = (pos % GRID_W).astype(f32)
    n_freq = QK_ROPE // 4
    inv = 1.0 / (ROPE_THETA ** (jnp.arange(n_freq, dtype=f32) / n_freq))
    ang = jnp.concatenate([row[:, None] * inv, col[:, None] * inv], -1)
    cos, sin = jnp.cos(ang), jnp.sin(ang)
    pad = LANES - QK_ROPE
    cos_t = jnp.concatenate([cos, cos, jnp.ones((T_S, pad), f32)], -1)
    sin_t = jnp.concatenate([-sin, sin, jnp.zeros((T_S, pad), f32)], -1)
    cos_t = jnp.concatenate([cos_t, jnp.ones((TM, LANES), f32)], 0)
    sin_t = jnp.concatenate([sin_t, jnp.zeros((TM, LANES), f32)], 0)
    return cos_t, sin_t


def _block_diag(w):
    eye = jnp.eye(LRU_HEADS, dtype=w.dtype)
    return jnp.einsum('hij,hg->higj', w, eye).reshape(LRU_W, LRU_W)


def kernel(x_prompt, x_sample, cache_ckv, cache_krope, state_lru, c, c_ctx, ada_w, ada_b, norm_mix_g, w_in, mla_qa_g, mla_w_qb, mla_q_g, mla_kva_g, mla_w_kvb, mla_k_g, lru_conv_w, lru_conv_b, lru_w_a, lru_b_a, lru_w_i, lru_b_i, lru_lam, cm_dw_w, cm_dw_b, cm_ln_g, cm_ln_b, grp_g, w_out, norm_ffn_g, peer_w_q, peer_keys, peer_u, peer_v):
    cs = jnp.concatenate([c_ctx[None, :], c, jnp.zeros((8 - 1 - NB_S, D), f32)], 0)
    mod = _modulation(cs, ada_w, ada_b).reshape(L, 8, 6, D)
    cos_t, sin_t = _rope_tables()
    x = jnp.concatenate([x_prompt.reshape(NP, D), x_sample.reshape(NS, D)], 0)
    zeros_h0 = jnp.zeros((NB_P, 2, LRU_W), f32)

    peer_out, x1 = None, None
    ckv_list, kr_list, h_list = [], [], []
    for l in range(L):
        wi = w_in[l]
        w_in_p = jnp.concatenate([wi[:, :832], jnp.zeros((D, LANES - QK_ROPE), f32), wi[:, 832:]], 1).astype(bf16)
        w_qb_p = jnp.pad(mla_w_qb[l].reshape(Q_LORA, HEADS, QK_HEAD),
                         ((0, 0), (0, 0), (0, HEAD_PAD - QK_HEAD))).reshape(Q_LORA, HEADS * HEAD_PAD).astype(bf16)
        qg_p = jnp.tile(jnp.pad(mla_q_g[l], (0, HEAD_PAD - QK_HEAD)), HEADS)[None, :]
        kgn = mla_k_g[l][None, :QK_NOPE]
        kgr = jnp.pad(mla_k_g[l][QK_NOPE:], (0, LANES - QK_ROPE))[None, :]
        wg = jnp.concatenate([_block_diag(lru_w_a[l, 0]), _block_diag(lru_w_a[l, 1]),
                              _block_diag(lru_w_i[l, 0]), _block_diag(lru_w_i[l, 1])], 1).astype(bf16)
        bg = jnp.concatenate([lru_b_a[l, 0], lru_b_a[l, 1], lru_b_i[l, 0], lru_b_i[l, 1]])[None, :]
        g_att = grp_g[l][None, :1024]
        g_lru = grp_g[l][None, 1024:1536]
        g_conv = grp_g[l][None, 1536:]
        seq_w = [lru_conv_w[l], lru_conv_b[l][None, :], wg, bg, lru_lam[l], g_lru,
                 cm_dw_w[l], cm_dw_b[l][None, :], cm_ln_g[l][None, :], cm_ln_b[l][None, :], g_conv]

        x, (qn, ckv, zkr, zx, zg, zc) = _inproj(
            x1 if l else x, peer_out, mod[l - 1] if l else None, mod[l],
            norm_mix_g[l][None, :], w_in_p, mla_qa_g[l][None, :], mla_kva_g[l][None, :])
        ckv_list.append(ckv[:NP].reshape(NB_P, T_P, KV_LORA))
        kr_list.append(zkr[:NP, :QK_ROPE].reshape(NB_P, T_P, QK_ROPE))

        q = _qproj(qn, w_qb_p, qg_p, cos_t, sin_t)
        ckv_all = jnp.concatenate(
            [jnp.concatenate([cache_ckv[:, l], ckv[NP:].reshape(NB_S, T_S, KV_LORA)], 1).reshape(NB_S * T_KV, KV_LORA),
             ckv[:NP]], 0)
        kr_ctx = jnp.pad(cache_krope[:, l], ((0, 0), (0, 0), (0, LANES - QK_ROPE)))
        kr_all = jnp.concatenate(
            [jnp.concatenate([kr_ctx, zkr[NP:].reshape(NB_S, T_S, LANES)], 1).reshape(NB_S * T_KV, LANES),
             zkr[:NP]], 0)
        k, v = _kvproj(ckv_all, kr_all, mla_w_kvb[l].astype(bf16), kgn, kgr, cos_t, sin_t)

        attn_p = _attention(q, k, v, NB_P, T_P, T_P, T_P, 0, NB_S * T_KV // T_P, NP)
        attn_s = _attention(q, k, v, NB_S, T_S, TQ_S, T_KV, NP // TQ_S, 0, NS)
        attn = jnp.concatenate([attn_p, attn_s], 0)

        lru_p, conv_p, hfin = _seqmix(zx, zg, zc, zeros_h0, NB_P, T_P, 0, seq_w)
        lru_s, conv_s, _ = _seqmix(zx, zg, zc, state_lru[:, l], NB_S, T_S, NP // T_S, seq_w)
        h_list.append(hfin)
        lru_n = jnp.concatenate([lru_p, lru_s], 0)
        conv_n = jnp.concatenate([conv_p, conv_s], 0)

        x1, hf = _outproj(attn, lru_n, conv_n, x, mod[l], g_att, w_out[l].astype(bf16), norm_ffn_g[l][None, :])
        st = _peerq(hf, peer_w_q[l].astype(bf16),
                    peer_keys[l].reshape(2 * PEER_HEADS, N_KEYS, N_KEYS).astype(bf16))
        rank2, w2, cnt1, w1 = _topk(st)
        peer_out = _peer(hf, peer_u[l].astype(bf16), peer_v[l].T.astype(bf16), rank2, w2, cnt1, w1)

    y = _residual(x1, peer_out, mod[L - 1])
    new_ckv = jnp.stack(ckv_list, 1)
    new_krope = jnp.stack(kr_list, 1)
    new_lru = jnp.stack(h_list, 1)
    return (y[:NP].reshape(NB_P, T_P, D), y[NP:].reshape(NB_S, T_S, D), new_ckv, new_krope, new_lru)
```

```python
import functools

import jax
import jax.numpy as jnp
from jax import lax
from jax.experimental import pallas as pl
from jax.experimental.pallas import tpu as pltpu

f32 = jnp.float32
bf16 = jnp.bfloat16
i32 = jnp.int32

D = 2048
NB_P, T_P = 16, 256
NB_S, T_S = 4, 1024
T_CTX = 512
NP = NB_P * T_P
NS = NB_S * T_S
NT = NP + NS
T_KV = T_CTX + T_S
NKV = NB_S * T_KV + NP
L = 2
HEADS = 8
QK_NOPE, QK_ROPE, V_HEAD = 128, 64, 128
QK_HEAD = QK_NOPE + QK_ROPE
Q_LORA, KV_LORA = 512, 256
LRU_W, CONV_W = 512, 512
LRU_HEADS, LRU_HD = 8, 64
LRU_K, CM_K = 4, 31
RG_C = 8.0
PEER_HEADS, N_KEYS, TOPK = 8, 128, 16
N_EXPERTS = N_KEYS * N_KEYS
EPS = 1e-6
LN_EPS = 1e-5
ROPE_THETA = 10000.0
GRID_W = 64

LANES = 128
HEAD_PAD = 256
IN_W_PAD = 2944
TM = 512
TQ_S = 512
TL = 128
PEER_TM = 512
PEER_TE = 1024
ROWS_PER_TE = PEER_TE // N_KEYS
VMEM_LIMIT = 56 * 1024 * 1024


def _cparams(sem):
    return pltpu.CompilerParams(dimension_semantics=sem, vmem_limit_bytes=VMEM_LIMIT)


def _mod_row(i, tm):
    return jnp.where(i < NP // tm, 0, 1 + (i * tm - NP) // T_S)


def _rms(x, g):
    return (x * lax.rsqrt(jnp.mean(x * x, -1, keepdims=True) + EPS)) * g


def _gelu_tanh(x):
    return 0.5 * x * (1.0 + jnp.tanh(0.7978845608028654 * (x + 0.044715 * (x * x * x))))


def _sigmoid(x):
    return 1.0 / (1.0 + jnp.exp(-x))


def _rope(a, cos, sin):
    lane = lax.broadcasted_iota(i32, a.shape, 1)
    half = QK_ROPE // 2
    sw = jnp.where(lane < half, pltpu.roll(a, LANES - half, 1), pltpu.roll(a, half, 1))
    return a * cos + sw * sin


def _mod_kernel(c_ref, w_ref, b_ref, o_ref):
    c = c_ref[...]
    a = (c * _sigmoid(c)).astype(bf16)
    o_ref[...] = jnp.dot(a, w_ref[...].astype(bf16), preferred_element_type=f32) + b_ref[...]


def _modulation(cs, ada_w, ada_b):
    tn = 1536
    return pl.pallas_call(
        _mod_kernel,
        out_shape=jax.ShapeDtypeStruct((L, 8, 6 * D), f32),
        grid=(L, 6 * D // tn),
        in_specs=[pl.BlockSpec((8, D), lambda l, j: (0, 0)),
                  pl.BlockSpec((None, D, tn), lambda l, j: (l, 0, j)),
                  pl.BlockSpec((None, 1, tn), lambda l, j: (l, 0, j))],
        out_specs=pl.BlockSpec((None, 8, tn), lambda l, j: (l, 0, j)),
        compiler_params=_cparams(("arbitrary", "arbitrary")),
        name="ada_mod",
    )(cs, ada_w, ada_b.reshape(L, 1, 6 * D))


def _inproj_kernel(has_res, *refs):
    if has_res:
        (x_ref, p_ref, mprev_ref, mod_ref, ng_ref, w_ref, qag_ref, kvag_ref,
         xo_ref, qn_ref, ckv_ref, zkr_ref, zx_ref, zg_ref, zc_ref) = refs
        x = x_ref[...] + mprev_ref[5:6, :] * p_ref[...]
        xo_ref[...] = x
    else:
        (x_ref, mod_ref, ng_ref, w_ref, qag_ref, kvag_ref,
         qn_ref, ckv_ref, zkr_ref, zx_ref, zg_ref, zc_ref) = refs
        x = x_ref[...]
    m = mod_ref[...]
    hn = _rms(x, ng_ref[...]) * (1.0 + m[1:2, :]) + m[0:1, :]
    hb = hn.astype(bf16)

    def seg(a, b):
        return jnp.dot(hb, w_ref[:, a:b], preferred_element_type=f32)

    qn_ref[...] = _rms(seg(0, 512), qag_ref[...]).astype(bf16)
    ckv_ref[...] = _rms(seg(512, 768), kvag_ref[...])
    zkr_ref[...] = seg(768, 896)
    zx_ref[...] = seg(896, 1408)
    zg_ref[...] = seg(1408, 1920)
    zc_ref[...] = seg(1920, 2944)


def _inproj(x, peer_out, mod_prev, mod_l, ng, w_in_p, qag, kvag):
    has_res = peer_out is not None
    row = lambda i: (i, 0)
    modspec = pl.BlockSpec((None, 6, D), lambda i: (_mod_row(i, TM), 0, 0))
    full = lambda shape: pl.BlockSpec(shape, lambda i: (0,) * len(shape))
    in_specs = [pl.BlockSpec((TM, D), row)]
    args = [x]
    if has_res:
        in_specs += [pl.BlockSpec((TM, D), row), modspec]
        args += [peer_out, mod_prev]
    in_specs += [modspec, full((1, D)), full((D, IN_W_PAD)), full((1, Q_LORA)), full((1, KV_LORA))]
    args += [mod_l, ng, w_in_p, qag, kvag]
    out_shape = [jax.ShapeDtypeStruct((NT, Q_LORA), bf16),
                 jax.ShapeDtypeStruct((NT, KV_LORA), f32),
                 jax.ShapeDtypeStruct((NT, LANES), f32),
                 jax.ShapeDtypeStruct((NT, LRU_W), f32),
                 jax.ShapeDtypeStruct((NT, LRU_W), f32),
                 jax.ShapeDtypeStruct((NT, 2 * CONV_W), f32)]
    out_specs = [pl.BlockSpec((TM, Q_LORA), row), pl.BlockSpec((TM, KV_LORA), row),
                 pl.BlockSpec((TM, LANES), row), pl.BlockSpec((TM, LRU_W), row),
                 pl.BlockSpec((TM, LRU_W), row), pl.BlockSpec((TM, 2 * CONV_W), row)]
    if has_res:
        out_shape = [jax.ShapeDtypeStruct((NT, D), f32)] + out_shape
        out_specs = [pl.BlockSpec((TM, D), row)] + out_specs
    outs = pl.pallas_call(
        functools.partial(_inproj_kernel, has_res),
        out_shape=out_shape, grid=(NT // TM,), in_specs=in_specs, out_specs=out_specs,
        compiler_params=_cparams(("arbitrary",)), name="in_proj",
    )(*args)
    if has_res:
        return outs[0], outs[1:]
    return x, outs


def _qproj_kernel(qn_ref, w_ref, g_ref, cos_ref, sin_ref, q_ref):
    qn = qn_ref[...]
    cos, sin = cos_ref[...], sin_ref[...]
    scale = QK_HEAD ** -0.5
    for h in range(HEADS):
        lo = h * HEAD_PAD
        qh = jnp.dot(qn, w_ref[:, lo:lo + HEAD_PAD], preferred_element_type=f32)
        r = lax.rsqrt(jnp.sum(qh * qh, -1, keepdims=True) * (1.0 / QK_HEAD) + EPS)
        qh = (qh * r) * g_ref[:, lo:lo + HEAD_PAD]
        q_ref[:, lo:lo + LANES] = (qh[:, :LANES] * scale).astype(bf16)
        q_ref[:, lo + LANES:lo + HEAD_PAD] = (_rope(qh[:, LANES:], cos, sin) * scale).astype(bf16)


def _qproj(qn, w_qb_p, qg_p, cos_t, sin_t):
    nblk_p = NP // TM
    tab = lambda i: (jnp.where(i < nblk_p, T_S // TM, (i - nblk_p) % (T_S // TM)), 0)
    return pl.pallas_call(
        _qproj_kernel,
        out_shape=jax.ShapeDtypeStruct((NT, HEADS * HEAD_PAD), bf16),
        grid=(NT // TM,),
        in_specs=[pl.BlockSpec((TM, Q_LORA), lambda i: (i, 0)),
                  pl.BlockSpec((Q_LORA, HEADS * HEAD_PAD), lambda i: (0, 0)),
                  pl.BlockSpec((1, HEADS * HEAD_PAD), lambda i: (0, 0)),
                  pl.BlockSpec((TM, LANES), tab), pl.BlockSpec((TM, LANES), tab)],
        out_specs=pl.BlockSpec((TM, HEADS * HEAD_PAD), lambda i: (i, 0)),
        compiler_params=_cparams(("arbitrary",)), name="q_proj",
    )(qn, w_qb_p, qg_p, cos_t, sin_t)


def _kvproj_kernel(ckv_ref, kr_ref, w_ref, gn_ref, gr_ref, cos_ref, sin_ref, k_ref, v_ref):
    ckv = ckv_ref[...].astype(bf16)
    kr = kr_ref[...]
    ssr = jnp.sum(kr * kr, -1, keepdims=True)
    cos, sin = cos_ref[...], sin_ref[...]
    for h in range(HEADS):
        lo = h * HEAD_PAD
        kv = jnp.dot(ckv, w_ref[:, lo:lo + HEAD_PAD], preferred_element_type=f32)
        kn = kv[:, :QK_NOPE]
        r = lax.rsqrt((jnp.sum(kn * kn, -1, keepdims=True) + ssr) * (1.0 / QK_HEAD) + EPS)
        k_ref[:, lo:lo + LANES] = ((kn * r) * gn_ref[...]).astype(bf16)
        k_ref[:, lo + LANES:lo + HEAD_PAD] = _rope((kr * r) * gr_ref[...], cos, sin).astype(bf16)
        v_ref[:, h * V_HEAD:(h + 1) * V_HEAD] = kv[:, QK_NOPE:].astype(bf16)


def _kvproj(ckv_all, kr_all, w_kvb, kgn, kgr, cos_t, sin_t):
    nblk_s = NB_S * T_KV // TM
    per = T_KV // TM
    ident = T_S // TM
    tab = lambda i: (jnp.where(i < nblk_s, (i % per + ident) % per, ident), 0)
    return pl.pallas_call(
        _kvproj_kernel,
        out_shape=[jax.ShapeDtypeStruct((NKV, HEADS * HEAD_PAD), bf16),
                   jax.ShapeDtypeStruct((NKV, HEADS * V_HEAD), bf16)],
        grid=(NKV // TM,),
        in_specs=[pl.BlockSpec((TM, KV_LORA), lambda i: (i, 0)),
                  pl.BlockSpec((TM, LANES), lambda i: (i, 0)),
                  pl.BlockSpec((KV_LORA, HEADS * HEAD_PAD), lambda i: (0, 0)),
                  pl.BlockSpec((1, LANES), lambda i: (0, 0)),
                  pl.BlockSpec((1, LANES), lambda i: (0, 0)),
                  pl.BlockSpec((TM, LANES), tab), pl.BlockSpec((TM, LANES), tab)],
        out_specs=[pl.BlockSpec((TM, HEADS * HEAD_PAD), lambda i: (i, 0)),
                   pl.BlockSpec((TM, HEADS * V_HEAD), lambda i: (i, 0))],
        compiler_params=_cparams(("arbitrary",)), name="kv_proj",
    )(ckv_all, kr_all, w_kvb, kgn, kgr, cos_t, sin_t)


def _attn_kernel(q_ref, k_ref, v_ref, o_ref):
    s = lax.dot_general(q_ref[...], k_ref[...], (((1,), (1,)), ((), ())), preferred_element_type=f32)
    p = jnp.exp(s - jnp.max(s, -1, keepdims=True))
    o = jnp.dot(p.astype(bf16), v_ref[...], preferred_element_type=f32)
    o_ref[...] = o / jnp.sum(p, -1, keepdims=True)


def _attention(q, k, v, nb, tq_total, tq, tk, q_blk0, kv_blk0, out_rows):
    nq = tq_total // tq
    return pl.pallas_call(
        _attn_kernel,
        out_shape=jax.ShapeDtypeStruct((out_rows, HEADS * V_HEAD), f32),
        grid=(nb, HEADS, nq),
        in_specs=[pl.BlockSpec((tq, HEAD_PAD), lambda b, h, i: (q_blk0 + b * nq + i, h)),
                  pl.BlockSpec((tk, HEAD_PAD), lambda b, h, i: (kv_blk0 + b, h)),
                  pl.BlockSpec((tk, V_HEAD), lambda b, h, i: (kv_blk0 + b, h))],
        out_specs=pl.BlockSpec((tq, V_HEAD), lambda b, h, i: (b * nq + i, h)),
        compiler_params=_cparams(("arbitrary", "arbitrary", "arbitrary")), name="attention",
    )(q, k, v)


def _seqmix_kernel(T, zx_ref, zg_ref, zc_ref, h0_ref, lcw_ref, lcb_ref, wg_ref, bg_ref, lam_ref,
                   glru_ref, cw_ref, cb_ref, lng_ref, lnb_ref, gconv_ref,
                   lru_ref, conv_ref, hfin_ref, xpad, a_s, u_s, hs, cpad):
    x = zx_ref[...]
    xpad[0:8, :] = jnp.zeros((8, LRU_W), f32)
    xpad[8:8 + T, :] = x
    xpad[8 + T:16 + T, :] = jnp.zeros((8, LRU_W), f32)
    xc = lcb_ref[...] + lcw_ref[0:1, :] * xpad[6:6 + T, :]
    for k in range(1, LRU_K):
        xc = xc + lcw_ref[k:k + 1, :] * xpad[6 + k:6 + k + T, :]
    g = jnp.dot(xc.astype(bf16), wg_ref[...], preferred_element_type=f32) + bg_ref[...]
    lam = lam_ref[...]
    sp = jnp.maximum(-lam, 0.0) + jnp.log(1.0 + jnp.exp(-jnp.abs(lam)))
    for d in range(2):
        r = _sigmoid(g[:, d * LRU_W:(d + 1) * LRU_W])
        gi = _sigmoid(g[:, (2 + d) * LRU_W:(3 + d) * LRU_W])
        log_a = (-RG_C * r) * sp[d:d + 1, :]
        a = jnp.exp(log_a)
        a_s[d] = a
        u_s[d] = (jnp.sqrt(1.0 - jnp.exp(2.0 * log_a)) * gi) * xc

    h0 = h0_ref[...]

    def blk(i, carry):
        hf, hb = carry
        r0 = pl.multiple_of(i * 8, 8)
        af = a_s[0, pl.ds(r0, 8), :]
        uf = u_s[0, pl.ds(r0, 8), :]
        rows = []
        for j in range(8):
            hf = af[j:j + 1, :] * hf + uf[j:j + 1, :]
            rows.append(hf)
        hs[0, pl.ds(r0, 8), :] = jnp.concatenate(rows, 0)
        r1 = pl.multiple_of(T - 8 - i * 8, 8)
        ab = a_s[1, pl.ds(r1, 8), :]
        ub = u_s[1, pl.ds(r1, 8), :]
        rows = [None] * 8
        for j in range(7, -1, -1):
            hb = ab[j:j + 1, :] * hb + ub[j:j + 1, :]
            rows[j] = hb
        hs[1, pl.ds(r1, 8), :] = jnp.concatenate(rows, 0)
        return hf, hb

    hf, hb = lax.fori_loop(0, T // 8, blk, (h0[0:1, :], h0[1:2, :]))
    hfin_ref[0:1, :] = hf
    hfin_ref[1:2, :] = hb
    y = (hs[0] + hs[1]) * _gelu_tanh(zg_ref[...])
    lru_ref[...] = _rms(y, glru_ref[...]).astype(bf16)

    hglu = zc_ref[:, 0:CONV_W] * _sigmoid(zc_ref[:, CONV_W:2 * CONV_W])
    cpad[0:16, :] = jnp.zeros((16, CONV_W), f32)
    cpad[16:16 + T, :] = hglu
    cpad[16 + T:32 + T, :] = jnp.zeros((16, CONV_W), f32)
    acc = cb_ref[...] + cw_ref[0:1, :] * cpad[1:1 + T, :]
    for k in range(1, CM_K):
        acc = acc + cw_ref[k:k + 1, :] * cpad[1 + k:1 + k + T, :]
    mu = jnp.mean(acc, -1, keepdims=True)
    cen = acc - mu
    var = jnp.mean(cen * cen, -1, keepdims=True)
    hln = (cen * lax.rsqrt(var + LN_EPS)) * lng_ref[...] + lnb_ref[...]
    hsl = hln * _sigmoid(hln)
    conv_ref[...] = _rms(hsl, gconv_ref[...]).astype(bf16)


def _seqmix(zx, zg, zc, h0, nb, T, blk0, wts):
    seq = lambda w: pl.BlockSpec((T, w), lambda b: (blk0 + b, 0))
    full = lambda a: pl.BlockSpec(a.shape, lambda b: (0,) * a.ndim)
    return pl.pallas_call(
        functools.partial(_seqmix_kernel, T),
        out_shape=[jax.ShapeDtypeStruct((nb * T, LRU_W), bf16),
                   jax.ShapeDtypeStruct((nb * T, CONV_W), bf16),
                   jax.ShapeDtypeStruct((nb, 2, LRU_W), f32)],
        grid=(nb,),
        in_specs=[seq(LRU_W), seq(LRU_W), seq(2 * CONV_W),
                  pl.BlockSpec((None, 2, LRU_W), lambda b: (b, 0, 0))] + [full(a) for a in wts],
        out_specs=[pl.BlockSpec((T, LRU_W), lambda b: (b, 0)),
                   pl.BlockSpec((T, CONV_W), lambda b: (b, 0)),
                   pl.BlockSpec((None, 2, LRU_W), lambda b: (b, 0, 0))],
        scratch_shapes=[pltpu.VMEM((T + 16, LRU_W), f32), pltpu.VMEM((2, T, LRU_W), f32),
                        pltpu.VMEM((2, T, LRU_W), f32), pltpu.VMEM((2, T, LRU_W), f32),
                        pltpu.VMEM((T + 32, CONV_W), f32)],
        compiler_params=_cparams(("arbitrary",)), name="seq_mix",
    )(zx, zg, zc, h0, *wts)


def _outproj_kernel(attn_ref, lru_ref, conv_ref, x_ref, mod_ref, gatt_ref, wo_ref, nf_ref,
                    x1_ref, hf_ref):
    an = _rms(attn_ref[...], gatt_ref[...]).astype(bf16)
    mixed = jnp.dot(an, wo_ref[0:1024, :], preferred_element_type=f32)
    mixed = mixed + jnp.dot(lru_ref[...], wo_ref[1024:1536, :], preferred_element_type=f32)
    mixed = mixed + jnp.dot(conv_ref[...], wo_ref[1536:2048, :], preferred_element_type=f32)
    m = mod_ref[...]
    x1 = x_ref[...] + m[2:3, :] * mixed
    x1_ref[...] = x1
    hf_ref[...] = (_rms(x1, nf_ref[...]) * (1.0 + m[4:5, :]) + m[3:4, :]).astype(bf16)


def _outproj(attn, lru_n, conv_n, x, mod_l, gatt, w_out, nf):
    row = lambda i: (i, 0)
    return pl.pallas_call(
        _outproj_kernel,
        out_shape=[jax.ShapeDtypeStruct((NT, D), f32), jax.ShapeDtypeStruct((NT, D), bf16)],
        grid=(NT // TM,),
        in_specs=[pl.BlockSpec((TM, HEADS * V_HEAD), row), pl.BlockSpec((TM, LRU_W), row),
                  pl.BlockSpec((TM, CONV_W), row), pl.BlockSpec((TM, D), row),
                  pl.BlockSpec((None, 6, D), lambda i: (_mod_row(i, TM), 0, 0)),
                  pl.BlockSpec((1, HEADS * V_HEAD), lambda i: (0, 0)),
                  pl.BlockSpec((D, D), lambda i: (0, 0)),
                  pl.BlockSpec((1, D), lambda i: (0, 0))],
        out_specs=[pl.BlockSpec((TM, D), row), pl.BlockSpec((TM, D), row)],
        compiler_params=_cparams(("arbitrary",)), name="out_proj",
    )(attn, lru_n, conv_n, x, mod_l, gatt, w_out, nf)


def _peerq_kernel(hf_ref, wq_ref, keys_ref, st_ref):
    q = jnp.dot(hf_ref[...], wq_ref[...], preferred_element_type=f32).astype(bf16)
    for hp in range(2 * PEER_HEADS):
        st_ref[hp] = lax.dot_general(keys_ref[hp], q[:, hp * N_KEYS:(hp + 1) * N_KEYS],
                                     (((1,), (1,)), ((), ())), preferred_element_type=f32)


def _peerq(hf, wq, keys):
    return pl.pallas_call(
        _peerq_kernel,
        out_shape=jax.ShapeDtypeStruct((2 * PEER_HEADS, N_KEYS, NT), f32),
        grid=(NT // TM,),
        in_specs=[pl.BlockSpec((TM, D), lambda i: (i, 0)),
                  pl.BlockSpec((D, D), lambda i: (0, 0)),
                  pl.BlockSpec((2 * PEER_HEADS, N_KEYS, N_KEYS), lambda i: (0, 0, 0))],
        out_specs=pl.BlockSpec((2 * PEER_HEADS, N_KEYS, TM), lambda i: (0, 0, i)),
        compiler_params=_cparams(("arbitrary",)), name="peer_query",
    )(hf, wq, keys)


_CAND_GROUPS = (((0, 8),), ((0, 8),), ((1, 8),), ((2, 5), (4, 3)), ((3, 4), (5, 2), (6, 2)), ((7, 2),), None)
N_CAND_ROWS = 8 * len(_CAND_GROUPS)


def _pick(sub, parts, fill):
    out = fill
    for start, val in parts:
        out = jnp.where(sub >= start, val, out)
    return out


def _topk_kernel(st_ref, rank2_ref, w2_ref, cnt1_ref, w1_ref, xs, tv, ti, cs, es, ss):
    n_arr = 2 * PEER_HEADS
    ninf = -jnp.inf
    kio = lax.broadcasted_iota(i32, (N_KEYS, TL), 0).astype(f32)
    sub = lax.broadcasted_iota(i32, (8, TL), 0)
    subf = sub.astype(f32)

    xs[...] = st_ref[...]

    def first(r, carry):
        for a in range(n_arr):
            x = xs[a]
            m = jnp.max(x, axis=0, keepdims=True)
            idx = jnp.min(jnp.where(x == m, kio, float(N_KEYS)), axis=0, keepdims=True)
            xs[a] = jnp.where(kio == idx, ninf, x)
            tv[a, pl.ds(r, 1), :] = m
            ti[a, pl.ds(r, 1), :] = idx
        return carry

    lax.fori_loop(0, TOPK, first, 0)

    pos_groups = []
    for gi, grp in enumerate(_CAND_GROUPS):
        if grp is None:
            pos_groups.append((subf + 8.0) * TOPK)
            continue
        parts, start = [], 0
        for r1, n in grp:
            r2_0 = 8.0 if gi == 1 else 0.0
            parts.append((start, r1 * TOPK + r2_0 + (subf - start)))
            start += n
        pos_groups.append(_pick(sub, parts[1:], parts[0][1]) if start == 8
                          else _pick(sub, [(start, 1e4)], _pick(sub, parts[1:], parts[0][1])))
    pos = jnp.concatenate(pos_groups, axis=0)

    for h in range(PEER_HEADS):
        t1 = tv[2 * h]
        t2 = tv[2 * h + 1]
        lo, hi = t2[0:8, :], t2[8:16, :]
        groups = []
        for gi, grp in enumerate(_CAND_GROUPS):
            if grp is None:
                groups.append(t1[8:16, :] + lo[0:1, :])
                continue
            src = hi if gi == 1 else lo
            a_parts, b_parts, start = [], [], 0
            for r1, n in grp:
                a_parts.append((start, t1[r1:r1 + 1, :]))
                b_parts.append((start, src if start == 0 else pltpu.roll(src, start, 0)))
                start += n
            c = _pick(sub, a_parts[1:], a_parts[0][1]) + _pick(sub, b_parts[1:], b_parts[0][1])
            groups.append(c if start == 8 else jnp.where(sub >= start, ninf, c))
        cand = jnp.concatenate(groups, axis=0)
        cs[h] = cand
        es[h] = jnp.exp(cand - cand[0:1, :])
    ss[...] = jnp.zeros(ss.shape, f32)

    def second(_, carry):
        for h in range(PEER_HEADS):
            cv = cs[h]
            m = jnp.max(cv, axis=0, keepdims=True)
            p = jnp.min(jnp.where(cv == m, pos, 1e5), axis=0, keepdims=True)
            hit = pos == p
            cs[h] = jnp.where(hit, ninf, cv)
            ss[h] = jnp.where(hit, 1.0, ss[h])
        return carry

    lax.fori_loop(0, TOPK, second, 0)

    def colsum(v):
        return jnp.sum(v, axis=0, keepdims=True)

    def finish(h, carry):
        sel = ss[h]
        z = colsum(sel * es[h])
        cnt = [None] * TOPK
        for gi, grp in enumerate(_CAND_GROUPS):
            sg = sel[8 * gi:8 * gi + 8, :]
            if grp is None:
                for i in range(8):
                    cnt[8 + i] = sg[i:i + 1, :]
                continue
            start = 0
            for r1, n in grp:
                part = colsum(jnp.where(sub >= start, jnp.where(sub < start + n, sg, 0.0), 0.0))
                cnt[r1] = part if cnt[r1] is None else cnt[r1] + part
                start += n
        i1 = ti[2 * h]
        i2 = ti[2 * h + 1]
        cnt1 = jnp.zeros((N_KEYS, TL), f32)
        rank2 = jnp.full((N_KEYS, TL), float(TOPK), f32)
        for r in range(TOPK):
            cnt1 = jnp.where(kio == i1[r:r + 1, :], cnt[r], cnt1)
            rank2 = jnp.where(kio == i2[r:r + 1, :], float(r), rank2)
        rank2_ref[h] = rank2.astype(bf16)
        cnt1_ref[h] = cnt1
        w2_ref[h] = jnp.exp(st_ref[2 * h + 1] - tv[2 * h + 1, 0:1, :]).astype(bf16)
        w1_ref[h] = jnp.exp(st_ref[2 * h] - tv[2 * h, 0:1, :]) / z
        return carry

    lax.fori_loop(0, PEER_HEADS, finish, 0)


def _topk(st):
    spec = pl.BlockSpec((PEER_HEADS, N_KEYS, TL), lambda i: (0, 0, i))
    shp = jax.ShapeDtypeStruct((PEER_HEADS, N_KEYS, NT), f32)
    shp_b = jax.ShapeDtypeStruct((PEER_HEADS, N_KEYS, NT), bf16)
    return pl.pallas_call(
        _topk_kernel,
        out_shape=[shp_b, shp_b, shp, shp],
        grid=(NT // TL,),
        in_specs=[pl.BlockSpec((2 * PEER_HEADS, N_KEYS, TL), lambda i: (0, 0, i))],
        out_specs=[spec, spec, spec, spec],
        scratch_shapes=[pltpu.VMEM((2 * PEER_HEADS, N_KEYS, TL), f32),
                        pltpu.VMEM((2 * PEER_HEADS, TOPK, TL), f32),
                        pltpu.VMEM((2 * PEER_HEADS, TOPK, TL), f32),
                        pltpu.VMEM((PEER_HEADS, N_CAND_ROWS, TL), f32),
                        pltpu.VMEM((PEER_HEADS, N_CAND_ROWS, TL), f32),
                        pltpu.VMEM((PEER_HEADS, N_CAND_ROWS, TL), f32)],
        compiler_params=_cparams(("arbitrary",)), name="peer_topk",
    )(st)


N_PEER_TILES = (NT // PEER_TM) * (N_EXPERTS // PEER_TE)
N_ETILES = N_EXPERTS // PEER_TE


def _peer_kernel(hf_ref, u_ref, vt_ref, rank2_ref, w2_ref, cnt1_ref, w1_ref, o_ref, acc, sbuf, at):
    s = pl.program_id(0)
    slot = s % 2
    last_e = jnp.maximum(s - 2, 0) % N_ETILES

    @pl.when(s == 0)
    def _():
        sbuf[...] = jnp.zeros(sbuf.shape, f32)
        at[...] = jnp.zeros(at.shape, bf16)

    @pl.when(last_e == 0)
    def _():
        acc[...] = jnp.zeros(acc.shape, f32)

    acc[...] += jnp.dot(vt_ref[...], at[1 - slot], preferred_element_type=f32)
    for e in range(ROWS_PER_TE):
        g = jnp.zeros((N_KEYS, PEER_TM), bf16)
        for h in range(PEER_HEADS):
            c = cnt1_ref[h, e:e + 1, :].astype(bf16)
            w = w1_ref[h, e:e + 1, :].astype(bf16)
            g = g + jnp.where(rank2_ref[h] < c, w2_ref[h] * w, 0.0)
        sv = sbuf[1 - slot, e * N_KEYS:(e + 1) * N_KEYS, :]
        at[slot, e * N_KEYS:(e + 1) * N_KEYS, :] = _gelu_tanh(sv).astype(bf16) * g
    sbuf[slot] = lax.dot_general(u_ref[...], hf_ref[...], (((1,), (1,)), ((), ())),
                                 preferred_element_type=f32)

    @pl.when((s > 1) & (last_e == N_ETILES - 1))
    def _():
        o_ref[...] = acc[...].T


def _peer(hf, u, vt, rank2, w2, cnt1, w1):
    tile = lambda s, lag: jnp.clip(s - lag, 0, N_PEER_TILES - 1)
    big = pl.BlockSpec((PEER_HEADS, N_KEYS, PEER_TM), lambda s: (0, 0, tile(s, 1) // N_ETILES))
    small = pl.BlockSpec((PEER_HEADS, ROWS_PER_TE, PEER_TM),
                         lambda s: (0, tile(s, 1) % N_ETILES, tile(s, 1) // N_ETILES))
    return pl.pallas_call(
        _peer_kernel,
        out_shape=jax.ShapeDtypeStruct((NT, D), f32),
        grid=(N_PEER_TILES + 2,),
        in_specs=[pl.BlockSpec((PEER_TM, D), lambda s: (tile(s, 0) // N_ETILES, 0)),
                  pl.BlockSpec((PEER_TE, D), lambda s: (tile(s, 0) % N_ETILES, 0)),
                  pl.BlockSpec((D, PEER_TE), lambda s: (0, tile(s, 2) % N_ETILES)),
                  big, big, small, small],
        out_specs=pl.BlockSpec((PEER_TM, D), lambda s: (tile(s, 2) // N_ETILES, 0)),
        scratch_shapes=[pltpu.VMEM((D, PEER_TM), f32), pltpu.VMEM((2, PEER_TE, PEER_TM), f32),
                        pltpu.VMEM((2, PEER_TE, PEER_TM), bf16)],
        compiler_params=_cparams(("arbitrary",)), name="peer_experts",
    )(hf, u, vt, rank2, w2, cnt1, w1)


def _residual_kernel(x_ref, p_ref, mod_ref, o_ref):
    o_ref[...] = x_ref[...] + mod_ref[5:6, :] * p_ref[...]


def _residual(x1, peer_out, mod_l):
    row = lambda i: (i, 0)
    return pl.pallas_call(
        _residual_kernel,
        out_shape=jax.ShapeDtypeStruct((NT, D), f32),
        grid=(NT // TM,),
        in_specs=[pl.BlockSpec((TM, D), row), pl.BlockSpec((TM, D), row),
                  pl.BlockSpec((None, 6, D), lambda i: (_mod_row(i, TM), 0, 0))],
        out_specs=pl.BlockSpec((TM, D), row),
        compiler_params=_cparams(("arbitrary",)), name="residual",
    )(x1, peer_out, mod_l)


def _rope_tables():
    pos = jnp.arange(T_S)
    row = (pos // GRID_W).astype(f32)
    col = (pos % GRID_W).astype(f32)
    n_freq = QK_ROPE // 4
    inv = 1.0 / (ROPE_THETA ** (jnp.arange(n_freq, dtype=f32) / n_freq))
    ang = jnp.concatenate([row[:, None] * inv, col[:, None] * inv], -1)
    cos, sin = jnp.cos(ang), jnp.sin(ang)
    pad = LANES - QK_ROPE
    cos_t = jnp.concatenate([cos, cos, jnp.ones((T_S, pad), f32)], -1)
    sin_t = jnp.concatenate([-sin, sin, jnp.zeros((T_S, pad), f32)], -1)
    cos_t = jnp.concatenate([cos_t, jnp.ones((TM, LANES), f32)], 0)
    sin_t = jnp.concatenate([sin_t, jnp.zeros((TM, LANES), f32)], 0)
    return cos_t, sin_t


def _block_diag(w):
    eye = jnp.eye(LRU_HEADS, dtype=w.dtype)
    return jnp.einsum('hij,hg->higj', w, eye).reshape(LRU_W, LRU_W)


def kernel(x_prompt, x_sample, cache_ckv, cache_krope, state_lru, c, c_ctx, ada_w, ada_b, norm_mix_g, w_in, mla_qa_g, mla_w_qb, mla_q_g, mla_kva_g, mla_w_kvb, mla_k_g, lru_conv_w, lru_conv_b, lru_w_a, lru_b_a, lru_w_i, lru_b_i, lru_lam, cm_dw_w, cm_dw_b, cm_ln_g, cm_ln_b, grp_g, w_out, norm_ffn_g, peer_w_q, peer_keys, peer_u, peer_v):
    cs = jnp.concatenate([c_ctx[None, :], c, jnp.zeros((8 - 1 - NB_S, D), f32)], 0)
    mod = _modulation(cs, ada_w, ada_b).reshape(L, 8, 6, D)
    cos_t, sin_t = _rope_tables()
    x = jnp.concatenate([x_prompt.reshape(NP, D), x_sample.reshape(NS, D)], 0)
    zeros_h0 = jnp.zeros((NB_P, 2, LRU_W), f32)

    peer_out, x1 = None, None
    ckv_list, kr_list, h_list = [], [], []
    for l in range(L):
        wi = w_in[l]
        w_in_p = jnp.concatenate([wi[:, :832], jnp.zeros((D, LANES - QK_ROPE), f32), wi[:, 832:]], 1).astype(bf16)
        w_qb_p = jnp.pad(mla_w_qb[l].reshape(Q_LORA, HEADS, QK_HEAD),
                         ((0, 0), (0, 0), (0, HEAD_PAD - QK_HEAD))).reshape(Q_LORA, HEADS * HEAD_PAD).astype(bf16)
        qg_p = jnp.tile(jnp.pad(mla_q_g[l], (0, HEAD_PAD - QK_HEAD)), HEADS)[None, :]
        kgn = mla_k_g[l][None, :QK_NOPE]
        kgr = jnp.pad(mla_k_g[l][QK_NOPE:], (0, LANES - QK_ROPE))[None, :]
        wg = jnp.concatenate([_block_diag(lru_w_a[l, 0]), _block_diag(lru_w_a[l, 1]),
                              _block_diag(lru_w_i[l, 0]), _block_diag(lru_w_i[l, 1])], 1).astype(bf16)
        bg = jnp.concatenate([lru_b_a[l, 0], lru_b_a[l, 1], lru_b_i[l, 0], lru_b_i[l, 1]])[None, :]
        g_att = grp_g[l][None, :1024]
        g_lru = grp_g[l][None, 1024:1536]
        g_conv = grp_g[l][None, 1536:]
        seq_w = [lru_conv_w[l], lru_conv_b[l][None, :], wg, bg, lru_lam[l], g_lru,
                 cm_dw_w[l], cm_dw_b[l][None, :], cm_ln_g[l][None, :], cm_ln_b[l][None, :], g_conv]

        x, (qn, ckv, zkr, zx, zg, zc) = _inproj(
            x1 if l else x, peer_out, mod[l - 1] if l else None, mod[l],
            norm_mix_g[l][None, :], w_in_p, mla_qa_g[l][None, :], mla_kva_g[l][None, :])
        ckv_list.append(ckv[:NP].reshape(NB_P, T_P, KV_LORA))
        kr_list.append(zkr[:NP, :QK_ROPE].reshape(NB_P, T_P, QK_ROPE))

        q = _qproj(qn, w_qb_p, qg_p, cos_t, sin_t)
        ckv_all = jnp.concatenate(
            [jnp.concatenate([cache_ckv[:, l], ckv[NP:].reshape(NB_S, T_S, KV_LORA)], 1).reshape(NB_S * T_KV, KV_LORA),
             ckv[:NP]], 0)
        kr_ctx = jnp.pad(cache_krope[:, l], ((0, 0), (0, 0), (0, LANES - QK_ROPE)))
        kr_all = jnp.concatenate(
            [jnp.concatenate([kr_ctx, zkr[NP:].reshape(NB_S, T_S, LANES)], 1).reshape(NB_S * T_KV, LANES),
             zkr[:NP]], 0)
        k, v = _kvproj(ckv_all, kr_all, mla_w_kvb[l].astype(bf16), kgn, kgr, cos_t, sin_t)

        attn_p = _attention(q, k, v, NB_P, T_P, T_P, T_P, 0, NB_S * T_KV // T_P, NP)
        attn_s = _attention(q, k, v, NB_S, T_S, TQ_S, T_KV, NP // TQ_S, 0, NS)
        attn = jnp.concatenate([attn_p, attn_s], 0)

        lru_p, conv_p, hfin = _seqmix(zx, zg, zc, zeros_h0, NB_P, T_P, 0, seq_w)
        lru_s, conv_s, _ = _seqmix(zx, zg, zc, state_lru[:, l], NB_S, T_S, NP // T_S, seq_w)
        h_list.append(hfin)
        lru_n = jnp.concatenate([lru_p, lru_s], 0)
        conv_n = jnp.concatenate([conv_p, conv_s], 0)

        x1, hf = _outproj(attn, lru_n, conv_n, x, mod[l], g_att, w_out[l].astype(bf16), norm_ffn_g[l][None, :])
        st = _peerq(hf, peer_w_q[l].astype(bf16),
                    peer_keys[l].reshape(2 * PEER_HEADS, N_KEYS, N_KEYS).astype(bf16))
        rank2, w2, cnt1, w1 = _topk(st)
        peer_out = _peer(hf, peer_u[l].astype(bf16), peer_v[l].T.astype(bf16), rank2, w2, cnt1, w1)

    y = _residual(x1, peer_out, mod[L - 1])
    new_ckv = jnp.stack(ckv_list, 1)
    new_krope = jnp.stack(kr_list, 1)
    new_lru = jnp.stack(h_list, 1)
    return (y[:NP].reshape(NB_P, T_P, D), y[NP:].reshape(NB_S, T_S, D), new_ckv, new_krope, new_lru)
```

```python
import functools

import jax
import jax.numpy as jnp
from jax import lax
from jax.experimental import pallas as pl
from jax.experimental.pallas import tpu as pltpu

f32 = jnp.float32
bf16 = jnp.bfloat16
i32 = jnp.int32

D = 2048
NB_P, T_P = 16, 256
NB_S, T_S = 4, 1024
T_CTX = 512
NP = NB_P * T_P
NS = NB_S * T_S
NT = NP + NS
T_KV = T_CTX + T_S
NKV = NB_S * T_KV + NP
L = 2
HEADS = 8
QK_NOPE, QK_ROPE, V_HEAD = 128, 64, 128
QK_HEAD = QK_NOPE + QK_ROPE
Q_LORA, KV_LORA = 512, 256
LRU_W, CONV_W = 512, 512
LRU_HEADS, LRU_HD = 8, 64
LRU_K, CM_K = 4, 31
RG_C = 8.0
PEER_HEADS, N_KEYS, TOPK = 8, 128, 16
N_EXPERTS = N_KEYS * N_KEYS
EPS = 1e-6
LN_EPS = 1e-5
ROPE_THETA = 10000.0
GRID_W = 64

LANES = 128
HEAD_PAD = 256
IN_W_PAD = 2944
TM = 512
TQ_S = 512
TL = 128
SEQ_CHUNK = 64
PEER_TM = 512
PEER_TE = 1024
ROWS_PER_TE = PEER_TE // N_KEYS
VMEM_LIMIT = 56 * 1024 * 1024


def _cparams(sem):
    return pltpu.CompilerParams(dimension_semantics=sem, vmem_limit_bytes=VMEM_LIMIT)


def _mod_row(i, tm):
    return jnp.where(i < NP // tm, 0, 1 + (i * tm - NP) // T_S)


def _rms(x, g):
    return (x * lax.rsqrt(jnp.mean(x * x, -1, keepdims=True) + EPS)) * g


def _gelu_tanh(x):
    return 0.5 * x * (1.0 + jnp.tanh(0.7978845608028654 * (x + 0.044715 * (x * x * x))))


def _gelu_tanh_bf16(x):
    z = x * (0.7978845608028654 + 0.035677408136300125 * (x * x))
    hb = x.astype(bf16) * 0.5
    return hb + hb * jnp.tanh(z.astype(bf16))


def _sigmoid(x):
    return 1.0 / (1.0 + jnp.exp(-x))


def _rope(a, cos, sin):
    lane = lax.broadcasted_iota(i32, a.shape, 1)
    half = QK_ROPE // 2
    sw = jnp.where(lane < half, pltpu.roll(a, LANES - half, 1), pltpu.roll(a, half, 1))
    return a * cos + sw * sin


def _mod_kernel(c_ref, w_ref, b_ref, o_ref):
    c = c_ref[...]
    a = (c * _sigmoid(c)).astype(bf16)
    o_ref[...] = jnp.dot(a, w_ref[...].astype(bf16), preferred_element_type=f32) + b_ref[...]


def _modulation(cs, ada_w, ada_b):
    tn = 1536
    return pl.pallas_call(
        _mod_kernel,
        out_shape=jax.ShapeDtypeStruct((L, 8, 6 * D), f32),
        grid=(L, 6 * D // tn),
        in_specs=[pl.BlockSpec((8, D), lambda l, j: (0, 0)),
                  pl.BlockSpec((None, D, tn), lambda l, j: (l, 0, j)),
                  pl.BlockSpec((None, 1, tn), lambda l, j: (l, 0, j))],
        out_specs=pl.BlockSpec((None, 8, tn), lambda l, j: (l, 0, j)),
        compiler_params=_cparams(("arbitrary", "arbitrary")),
        name="ada_mod",
    )(cs, ada_w, ada_b.reshape(L, 1, 6 * D))


def _inproj_kernel(has_res, *refs):
    if has_res:
        (x_ref, p_ref, mprev_ref, mod_ref, ng_ref, w_ref, qag_ref, kvag_ref,
         xo_ref, qn_ref, ckv_ref, zkr_ref, zx_ref, zg_ref, zc_ref) = refs
        x = x_ref[...] + mprev_ref[5:6, :] * p_ref[...]
        xo_ref[...] = x
    else:
        (x_ref, mod_ref, ng_ref, w_ref, qag_ref, kvag_ref,
         qn_ref, ckv_ref, zkr_ref, zx_ref, zg_ref, zc_ref) = refs
        x = x_ref[...]
    m = mod_ref[...]
    hn = _rms(x, ng_ref[...]) * (1.0 + m[1:2, :]) + m[0:1, :]
    hb = hn.astype(bf16)

    def seg(a, b):
        return jnp.dot(hb, w_ref[:, a:b], preferred_element_type=f32)

    qn_ref[...] = _rms(seg(0, 512), qag_ref[...]).astype(bf16)
    ckv_ref[...] = _rms(seg(512, 768), kvag_ref[...])
    zkr_ref[...] = seg(768, 896)
    zx_ref[...] = seg(896, 1408)
    zg_ref[...] = seg(1408, 1920)
    zc_ref[...] = seg(1920, 2944)


def _inproj(x, peer_out, mod_prev, mod_l, ng, w_in_p, qag, kvag):
    has_res = peer_out is not None
    row = lambda i: (i, 0)
    modspec = pl.BlockSpec((None, 6, D), lambda i: (_mod_row(i, TM), 0, 0))
    full = lambda shape: pl.BlockSpec(shape, lambda i: (0,) * len(shape))
    in_specs = [pl.BlockSpec((TM, D), row)]
    args = [x]
    if has_res:
        in_specs += [pl.BlockSpec((TM, D), row), modspec]
        args += [peer_out, mod_prev]
    in_specs += [modspec, full((1, D)), full((D, IN_W_PAD)), full((1, Q_LORA)), full((1, KV_LORA))]
    args += [mod_l, ng, w_in_p, qag, kvag]
    out_shape = [jax.ShapeDtypeStruct((NT, Q_LORA), bf16),
                 jax.ShapeDtypeStruct((NT, KV_LORA), f32),
                 jax.ShapeDtypeStruct((NT, LANES), f32),
                 jax.ShapeDtypeStruct((NT, LRU_W), f32),
                 jax.ShapeDtypeStruct((NT, LRU_W), f32),
                 jax.ShapeDtypeStruct((NT, 2 * CONV_W), f32)]
    out_specs = [pl.BlockSpec((TM, Q_LORA), row), pl.BlockSpec((TM, KV_LORA), row),
                 pl.BlockSpec((TM, LANES), row), pl.BlockSpec((TM, LRU_W), row),
                 pl.BlockSpec((TM, LRU_W), row), pl.BlockSpec((TM, 2 * CONV_W), row)]
    if has_res:
        out_shape = [jax.ShapeDtypeStruct((NT, D), f32)] + out_shape
        out_specs = [pl.BlockSpec((TM, D), row)] + out_specs
    outs = pl.pallas_call(
        functools.partial(_inproj_kernel, has_res),
        out_shape=out_shape, grid=(NT // TM,), in_specs=in_specs, out_specs=out_specs,
        compiler_params=_cparams(("arbitrary",)), name="in_proj",
    )(*args)
    if has_res:
        return outs[0], outs[1:]
    return x, outs


def _qproj_kernel(qn_ref, w_ref, g_ref, cos_ref, sin_ref, q_ref):
    qn = qn_ref[...]
    cos, sin = cos_ref[...], sin_ref[...]
    scale = QK_HEAD ** -0.5
    for h in range(HEADS):
        lo = h * HEAD_PAD
        qh = jnp.dot(qn, w_ref[:, lo:lo + HEAD_PAD], preferred_element_type=f32)
        r = lax.rsqrt(jnp.sum(qh * qh, -1, keepdims=True) * (1.0 / QK_HEAD) + EPS)
        qh = (qh * r) * g_ref[:, lo:lo + HEAD_PAD]
        q_ref[:, lo:lo + LANES] = (qh[:, :LANES] * scale).astype(bf16)
        q_ref[:, lo + LANES:lo + HEAD_PAD] = (_rope(qh[:, LANES:], cos, sin) * scale).astype(bf16)


def _qproj(qn, w_qb_p, qg_p, cos_t, sin_t):
    nblk_p = NP // TM
    tab = lambda i: (jnp.where(i < nblk_p, T_S // TM, (i - nblk_p) % (T_S // TM)), 0)
    return pl.pallas_call(
        _qproj_kernel,
        out_shape=jax.ShapeDtypeStruct((NT, HEADS * HEAD_PAD), bf16),
        grid=(NT // TM,),
        in_specs=[pl.BlockSpec((TM, Q_LORA), lambda i: (i, 0)),
                  pl.BlockSpec((Q_LORA, HEADS * HEAD_PAD), lambda i: (0, 0)),
                  pl.BlockSpec((1, HEADS * HEAD_PAD), lambda i: (0, 0)),
                  pl.BlockSpec((TM, LANES), tab), pl.BlockSpec((TM, LANES), tab)],
        out_specs=pl.BlockSpec((TM, HEADS * HEAD_PAD), lambda i: (i, 0)),
        compiler_params=_cparams(("arbitrary",)), name="q_proj",
    )(qn, w_qb_p, qg_p, cos_t, sin_t)


def _kvproj_kernel(ckv_ref, kr_ref, w_ref, gn_ref, gr_ref, cos_ref, sin_ref, k_ref, v_ref):
    ckv = ckv_ref[...].astype(bf16)
    kr = kr_ref[...]
    ssr = jnp.sum(kr * kr, -1, keepdims=True)
    cos, sin = cos_ref[...], sin_ref[...]
    for h in range(HEADS):
        lo = h * HEAD_PAD
        kv = jnp.dot(ckv, w_ref[:, lo:lo + HEAD_PAD], preferred_element_type=f32)
        kn = kv[:, :QK_NOPE]
        r = lax.rsqrt((jnp.sum(kn * kn, -1, keepdims=True) + ssr) * (1.0 / QK_HEAD) + EPS)
        k_ref[:, lo:lo + LANES] = ((kn * r) * gn_ref[...]).astype(bf16)
        k_ref[:, lo + LANES:lo + HEAD_PAD] = _rope((kr * r) * gr_ref[...], cos, sin).astype(bf16)
        v_ref[:, h * V_HEAD:(h + 1) * V_HEAD] = kv[:, QK_NOPE:].astype(bf16)


def _kvproj(ckv_all, kr_all, w_kvb, kgn, kgr, cos_t, sin_t):
    nblk_s = NB_S * T_KV // TM
    per = T_KV // TM
    ident = T_S // TM
    tab = lambda i: (jnp.where(i < nblk_s, (i % per + ident) % per, ident), 0)
    return pl.pallas_call(
        _kvproj_kernel,
        out_shape=[jax.ShapeDtypeStruct((NKV, HEADS * HEAD_PAD), bf16),
                   jax.ShapeDtypeStruct((NKV, HEADS * V_HEAD), bf16)],
        grid=(NKV // TM,),
        in_specs=[pl.BlockSpec((TM, KV_LORA), lambda i: (i, 0)),
                  pl.BlockSpec((TM, LANES), lambda i: (i, 0)),
                  pl.BlockSpec((KV_LORA, HEADS * HEAD_PAD), lambda i: (0, 0)),
                  pl.BlockSpec((1, LANES), lambda i: (0, 0)),
                  pl.BlockSpec((1, LANES), lambda i: (0, 0)),
                  pl.BlockSpec((TM, LANES), tab), pl.BlockSpec((TM, LANES), tab)],
        out_specs=[pl.BlockSpec((TM, HEADS * HEAD_PAD), lambda i: (i, 0)),
                   pl.BlockSpec((TM, HEADS * V_HEAD), lambda i: (i, 0))],
        compiler_params=_cparams(("arbitrary",)), name="kv_proj",
    )(ckv_all, kr_all, w_kvb, kgn, kgr, cos_t, sin_t)


def _attn_head(q, k, v):
    s = lax.dot_general(q, k, (((1,), (1,)), ((), ())), preferred_element_type=f32)
    p = jnp.exp(s - jnp.max(s, -1, keepdims=True))
    o = jnp.dot(p.astype(bf16), v, preferred_element_type=f32)
    return o / jnp.sum(p, -1, keepdims=True)


def _attn_kernel(q_ref, k_ref, v_ref, o_ref):
    o_ref[...] = _attn_head(q_ref[...], k_ref[...], v_ref[...])


def _attn_allheads_kernel(q_ref, k_ref, v_ref, o_ref):
    for h in range(HEADS):
        qk = slice(h * HEAD_PAD, (h + 1) * HEAD_PAD)
        hv = slice(h * V_HEAD, (h + 1) * V_HEAD)
        o_ref[:, hv] = _attn_head(q_ref[:, qk], k_ref[:, qk], v_ref[:, hv])


def _attention_ctx(q, k, v):
    kv0 = NB_S * T_KV // T_P
    return pl.pallas_call(
        _attn_allheads_kernel,
        out_shape=jax.ShapeDtypeStruct((NP, HEADS * V_HEAD), f32),
        grid=(NB_P,),
        in_specs=[pl.BlockSpec((T_P, HEADS * HEAD_PAD), lambda b: (b, 0)),
                  pl.BlockSpec((T_P, HEADS * HEAD_PAD), lambda b: (kv0 + b, 0)),
                  pl.BlockSpec((T_P, HEADS * V_HEAD), lambda b: (kv0 + b, 0))],
        out_specs=pl.BlockSpec((T_P, HEADS * V_HEAD), lambda b: (b, 0)),
        compiler_params=_cparams(("arbitrary",)), name="attention_ctx",
    )(q, k, v)


def _attention_lat(q, k, v):
    nq = T_S // TQ_S
    q0 = NP // TQ_S
    return pl.pallas_call(
        _attn_kernel,
        out_shape=jax.ShapeDtypeStruct((NS, HEADS * V_HEAD), f32),
        grid=(NB_S, HEADS, nq),
        in_specs=[pl.BlockSpec((TQ_S, HEAD_PAD), lambda b, h, i: (q0 + b * nq + i, h)),
                  pl.BlockSpec((T_KV, HEAD_PAD), lambda b, h, i: (b, h)),
                  pl.BlockSpec((T_KV, V_HEAD), lambda b, h, i: (b, h))],
        out_specs=pl.BlockSpec((TQ_S, V_HEAD), lambda b, h, i: (b * nq + i, h)),
        compiler_params=_cparams(("arbitrary", "arbitrary", "arbitrary")), name="attention_lat",
    )(q, k, v)


def _seqmix_kernel(T, zx_ref, zg_ref, zc_ref, h0_ref, lcw_ref, lcb_ref, wg_ref, bg_ref, lam_ref,
                   glru_ref, cw_ref, cb_ref, lng_ref, lnb_ref, gconv_ref,
                   lru_ref, conv_ref, hfin_ref, xpad, a_s, u_s, hs, cpad, xc_s, g_s, wbuf):
    n_chunks = T // SEQ_CHUNK

    def chunked(body, static=False):
        if static:
            for c in range(n_chunks):
                body(c * SEQ_CHUNK)
            return

        def step(c, carry):
            body(pl.multiple_of(c * SEQ_CHUNK, SEQ_CHUNK))
            return carry
        lax.fori_loop(0, n_chunks, step, 0)

    xpad[0:8, :] = jnp.zeros((8, LRU_W), f32)
    xpad[8:8 + T, :] = zx_ref[...]
    xpad[8 + T:16 + T, :] = jnp.zeros((8, LRU_W), f32)

    def lru_conv(r0):
        xc = lcb_ref[...] + lcw_ref[0:1, :] * xpad[pl.ds(r0 + 6, SEQ_CHUNK), :]
        for k in range(1, LRU_K):
            xc = xc + lcw_ref[k:k + 1, :] * xpad[pl.ds(r0 + 6 + k, SEQ_CHUNK), :]
        xc_s[pl.ds(r0, SEQ_CHUNK), :] = xc

    chunked(lru_conv, static=True)
    xcb = xc_s[...].astype(bf16)
    for j in range(4):
        cols = slice(j * LRU_W, (j + 1) * LRU_W)
        g_s[:, cols] = jnp.dot(xcb, wg_ref[:, cols], preferred_element_type=f32) + bg_ref[:, cols]
    lam = lam_ref[...]
    sp = jnp.maximum(-lam, 0.0) + jnp.log(1.0 + jnp.exp(-jnp.abs(lam)))

    def lru_coeffs(r0):
        rows = pl.ds(r0, SEQ_CHUNK)
        xc = xc_s[rows, :]
        for d in range(2):
            r = _sigmoid(g_s[rows, d * LRU_W:(d + 1) * LRU_W])
            gi = _sigmoid(g_s[rows, (2 + d) * LRU_W:(3 + d) * LRU_W])
            a = jnp.exp((-RG_C * r) * sp[d:d + 1, :])
            a_s[d, rows, :] = a
            u_s[d, rows, :] = (jnp.sqrt(1.0 - a * a) * gi) * xc

    chunked(lru_coeffs)
    h0 = h0_ref[...]

    rowi = lax.broadcasted_iota(i32, (8, LRU_W), 0)

    def tile_scan(a, u, reverse):
        for d in (1, 2, 4):
            if reverse:
                keep = rowi < 8 - d
                sa, su = pltpu.roll(a, 8 - d, 0), pltpu.roll(u, 8 - d, 0)
            else:
                keep = rowi >= d
                sa, su = pltpu.roll(a, d, 0), pltpu.roll(u, d, 0)
            u = u + a * jnp.where(keep, su, 0.0)
            a = a * jnp.where(keep, sa, 1.0)
        return a, u

    def blk(i, carry):
        hf, hb = carry
        r0 = pl.multiple_of(i * 8, 8)
        ca, cu = tile_scan(a_s[0, pl.ds(r0, 8), :], u_s[0, pl.ds(r0, 8), :], False)
        rows = ca * hf + cu
        hs[0, pl.ds(r0, 8), :] = rows
        hf = rows[7:8, :]
        r1 = pl.multiple_of(T - 8 - i * 8, 8)
        ca, cu = tile_scan(a_s[1, pl.ds(r1, 8), :], u_s[1, pl.ds(r1, 8), :], True)
        rows = ca * hb + cu
        hs[1, pl.ds(r1, 8), :] = rows
        hb = rows[0:1, :]
        return hf, hb

    hf, hb = lax.fori_loop(0, T // 8, blk, (h0[0:1, :], h0[1:2, :]))
    hfin_ref[0:1, :] = hf
    hfin_ref[1:2, :] = hb
    def lru_out(r0):
        rows = pl.ds(r0, SEQ_CHUNK)
        y = (hs[0, rows, :] + hs[1, rows, :]) * _gelu_tanh(zg_ref[rows, :])
        lru_ref[rows, :] = _rms(y, glru_ref[...]).astype(bf16)

    chunked(lru_out)

    cpad[0:16, :] = jnp.zeros((16, CONV_W), f32)
    cpad[16 + T:32 + T, :] = jnp.zeros((16, CONV_W), f32)

    def glu(r0):
        rows = pl.ds(r0, SEQ_CHUNK)
        cpad[pl.ds(r0 + 16, SEQ_CHUNK), :] = zc_ref[rows, 0:CONV_W] * _sigmoid(zc_ref[rows, CONV_W:2 * CONV_W])

    chunked(glu)

    def conv_out(r0):
        acc = cb_ref[...]
        for res in range(8):
            taps = [k for k in range(CM_K) if (1 + k) % 8 == res]
            span = max(1 + k - res for k in taps) + SEQ_CHUNK
            wbuf[res, 0:span, :] = cpad[pl.ds(r0 + res, span), :]
            for k in taps:
                off = 1 + k - res
                acc = acc + cw_ref[k:k + 1, :] * wbuf[res, off:off + SEQ_CHUNK, :]
        mu = jnp.mean(acc, -1, keepdims=True)
        cen = acc - mu
        var = jnp.mean(cen * cen, -1, keepdims=True)
        hln = (cen * lax.rsqrt(var + LN_EPS)) * lng_ref[...] + lnb_ref[...]
        hsl = hln * _sigmoid(hln)
        conv_ref[pl.ds(r0, SEQ_CHUNK), :] = _rms(hsl, gconv_ref[...]).astype(bf16)

    chunked(conv_out, static=True)


def _seqmix(zx, zg, zc, h0, nb, T, blk0, wts):
    seq = lambda w: pl.BlockSpec((T, w), lambda b: (blk0 + b, 0))
    full = lambda a: pl.BlockSpec(a.shape, lambda b: (0,) * a.ndim)
    return pl.pallas_call(
        functools.partial(_seqmix_kernel, T),
        out_shape=[jax.ShapeDtypeStruct((nb * T, LRU_W), bf16),
                   jax.ShapeDtypeStruct((nb * T, CONV_W), bf16),
                   jax.ShapeDtypeStruct((nb, 2, LRU_W), f32)],
        grid=(nb,),
        in_specs=[seq(LRU_W), seq(LRU_W), seq(2 * CONV_W),
                  pl.BlockSpec((None, 2, LRU_W), lambda b: (b, 0, 0))] + [full(a) for a in wts],
        out_specs=[pl.BlockSpec((T, LRU_W), lambda b: (b, 0)),
                   pl.BlockSpec((T, CONV_W), lambda b: (b, 0)),
                   pl.BlockSpec((None, 2, LRU_W), lambda b: (b, 0, 0))],
        scratch_shapes=[pltpu.VMEM((T + 16, LRU_W), f32), pltpu.VMEM((2, T, LRU_W), f32),
                        pltpu.VMEM((2, T, LRU_W), f32), pltpu.VMEM((2, T, LRU_W), f32),
                        pltpu.VMEM((T + 32, CONV_W), f32), pltpu.VMEM((T, LRU_W), f32),
                        pltpu.VMEM((T, 4 * LRU_W), f32),
                        pltpu.VMEM((8, SEQ_CHUNK + 24, CONV_W), f32)],
        compiler_params=_cparams(("arbitrary",)), name="seq_mix",
    )(zx, zg, zc, h0, *wts)


def _outproj_kernel(attn_p_ref, attn_s_ref, lru_p_ref, lru_s_ref, conv_p_ref, conv_s_ref,
                    x_ref, mod_ref, gatt_ref, wo_ref, nf_ref, x1_ref, hf_ref):
    is_ctx = pl.program_id(0) < NP // TM
    attn = jnp.where(is_ctx, attn_p_ref[...], attn_s_ref[...])
    lru = jnp.where(is_ctx, lru_p_ref[...], lru_s_ref[...])
    conv = jnp.where(is_ctx, conv_p_ref[...], conv_s_ref[...])
    an = _rms(attn, gatt_ref[...]).astype(bf16)
    mixed = jnp.dot(an, wo_ref[0:1024, :], preferred_element_type=f32)
    mixed = mixed + jnp.dot(lru, wo_ref[1024:1536, :], preferred_element_type=f32)
    mixed = mixed + jnp.dot(conv, wo_ref[1536:2048, :], preferred_element_type=f32)
    m = mod_ref[...]
    x1 = x_ref[...] + m[2:3, :] * mixed
    x1_ref[...] = x1
    hf_ref[...] = (_rms(x1, nf_ref[...]) * (1.0 + m[4:5, :]) + m[3:4, :]).astype(bf16)


def _outproj(attn_p, attn_s, lru_p, lru_s, conv_p, conv_s, x, mod_l, gatt, w_out, nf):
    row = lambda i: (i, 0)
    nblk_p = NP // TM
    ctx = lambda i: (jnp.minimum(i, nblk_p - 1), 0)
    lat = lambda i: (jnp.maximum(i - nblk_p, 0), 0)
    return pl.pallas_call(
        _outproj_kernel,
        out_shape=[jax.ShapeDtypeStruct((NT, D), f32), jax.ShapeDtypeStruct((NT, D), bf16)],
        grid=(NT // TM,),
        in_specs=[pl.BlockSpec((TM, HEADS * V_HEAD), ctx), pl.BlockSpec((TM, HEADS * V_HEAD), lat),
                  pl.BlockSpec((TM, LRU_W), ctx), pl.BlockSpec((TM, LRU_W), lat),
                  pl.BlockSpec((TM, CONV_W), ctx), pl.BlockSpec((TM, CONV_W), lat),
                  pl.BlockSpec((TM, D), row),
                  pl.BlockSpec((None, 6, D), lambda i: (_mod_row(i, TM), 0, 0)),
                  pl.BlockSpec((1, HEADS * V_HEAD), lambda i: (0, 0)),
                  pl.BlockSpec((D, D), lambda i: (0, 0)),
                  pl.BlockSpec((1, D), lambda i: (0, 0))],
        out_specs=[pl.BlockSpec((TM, D), row), pl.BlockSpec((TM, D), row)],
        compiler_params=_cparams(("arbitrary",)), name="out_proj",
    )(attn_p, attn_s, lru_p, lru_s, conv_p, conv_s, x, mod_l, gatt, w_out, nf)


def _peerq_kernel(hf_ref, wq_ref, keys_ref, st_ref):
    q = jnp.dot(hf_ref[...], wq_ref[...], preferred_element_type=f32).astype(bf16)
    for hp in range(2 * PEER_HEADS):
        st_ref[hp] = lax.dot_general(keys_ref[hp], q[:, hp * N_KEYS:(hp + 1) * N_KEYS],
                                     (((1,), (1,)), ((), ())), preferred_element_type=f32)


def _peerq(hf, wq, keys):
    return pl.pallas_call(
        _peerq_kernel,
        out_shape=jax.ShapeDtypeStruct((2 * PEER_HEADS, N_KEYS, NT), f32),
        grid=(NT // TM,),
        in_specs=[pl.BlockSpec((TM, D), lambda i: (i, 0)),
                  pl.BlockSpec((D, D), lambda i: (0, 0)),
                  pl.BlockSpec((2 * PEER_HEADS, N_KEYS, N_KEYS), lambda i: (0, 0, 0))],
        out_specs=pl.BlockSpec((2 * PEER_HEADS, N_KEYS, TM), lambda i: (0, 0, i)),
        compiler_params=_cparams(("arbitrary",)), name="peer_query",
    )(hf, wq, keys)


_CAND_GROUPS = (((0, 8),), ((0, 8),), ((1, 8),), ((2, 5), (4, 3)), ((3, 4), (5, 2), (6, 2)), ((7, 2),), None)
N_CAND_ROWS = 8 * len(_CAND_GROUPS)


def _pick(sub, parts, fill):
    out = fill
    for start, val in parts:
        out = jnp.where(sub >= start, val, out)
    return out


def _topk_kernel(st_ref, rank2_ref, w2_ref, cnt1_ref, w1_ref, xs, tv, ti, cs, es, ss):
    n_arr = 2 * PEER_HEADS
    ninf = -jnp.inf
    kio = lax.broadcasted_iota(i32, (N_KEYS, TL), 0).astype(f32)
    sub = lax.broadcasted_iota(i32, (8, TL), 0)
    subf = sub.astype(f32)

    xs[...] = st_ref[...]

    def first(r, carry):
        for a in range(n_arr):
            x = xs[a]
            m = jnp.max(x, axis=0, keepdims=True)
            idx = jnp.min(jnp.where(x == m, kio, float(N_KEYS)), axis=0, keepdims=True)
            xs[a] = jnp.where(kio == idx, ninf, x)
            tv[a, pl.ds(r, 1), :] = m
            ti[a, pl.ds(r, 1), :] = idx
        return carry

    lax.fori_loop(0, TOPK, first, 0)

    pos_groups = []
    for gi, grp in enumerate(_CAND_GROUPS):
        if grp is None:
            pos_groups.append((subf + 8.0) * TOPK)
            continue
        parts, start = [], 0
        for r1, n in grp:
            r2_0 = 8.0 if gi == 1 else 0.0
            parts.append((start, r1 * TOPK + r2_0 + (subf - start)))
            start += n
        pos_groups.append(_pick(sub, parts[1:], parts[0][1]) if start == 8
                          else _pick(sub, [(start, 1e4)], _pick(sub, parts[1:], parts[0][1])))
    pos = jnp.concatenate(pos_groups, axis=0)

    for h in range(PEER_HEADS):
        t1 = tv[2 * h]
        t2 = tv[2 * h + 1]
        lo, hi = t2[0:8, :], t2[8:16, :]
        groups = []
        for gi, grp in enumerate(_CAND_GROUPS):
            if grp is None:
                groups.append(t1[8:16, :] + lo[0:1, :])
                continue
            src = hi if gi == 1 else lo
            a_parts, b_parts, start = [], [], 0
            for r1, n in grp:
                a_parts.append((start, t1[r1:r1 + 1, :]))
                b_parts.append((start, src if start == 0 else pltpu.roll(src, start, 0)))
                start += n
            c = _pick(sub, a_parts[1:], a_parts[0][1]) + _pick(sub, b_parts[1:], b_parts[0][1])
            groups.append(c if start == 8 else jnp.where(sub >= start, ninf, c))
        cand = jnp.concatenate(groups, axis=0)
        cs[h] = cand
        es[h] = jnp.exp(cand - cand[0:1, :])
    ss[...] = jnp.zeros(ss.shape, f32)

    def second(_, carry):
        for h in range(PEER_HEADS):
            cv = cs[h]
            m = jnp.max(cv, axis=0, keepdims=True)
            p = jnp.min(jnp.where(cv == m, pos, 1e5), axis=0, keepdims=True)
            hit = pos == p
            cs[h] = jnp.where(hit, ninf, cv)
            ss[h] = jnp.where(hit, 1.0, ss[h])
        return carry

    lax.fori_loop(0, TOPK, second, 0)

    def colsum(v):
        return jnp.sum(v, axis=0, keepdims=True)

    def finish(h, carry):
        sel = ss[h]
        z = colsum(sel * es[h])
        cnt = [None] * TOPK
        for gi, grp in enumerate(_CAND_GROUPS):
            sg = sel[8 * gi:8 * gi + 8, :]
            if grp is None:
                for i in range(8):
                    cnt[8 + i] = sg[i:i + 1, :]
                continue
            start = 0
            for r1, n in grp:
                part = colsum(jnp.where(sub >= start, jnp.where(sub < start + n, sg, 0.0), 0.0))
                cnt[r1] = part if cnt[r1] is None else cnt[r1] + part
                start += n
        i1 = ti[2 * h]
        i2 = ti[2 * h + 1]
        cnt1 = jnp.zeros((N_KEYS, TL), f32)
        rank2 = jnp.full((N_KEYS, TL), float(TOPK), f32)
        for r in range(TOPK):
            cnt1 = jnp.where(kio == i1[r:r + 1, :], cnt[r], cnt1)
            rank2 = jnp.where(kio == i2[r:r + 1, :], float(r), rank2)
        def grouped(v):
            return v.reshape(N_KEYS // 16, 16, TL).astype(bf16)

        rank2_ref[h] = grouped(rank2)
        cnt1_ref[h] = cnt1
        w2_ref[h] = grouped(jnp.exp(st_ref[2 * h + 1] - tv[2 * h + 1, 0:1, :]))
        w1_ref[h] = jnp.exp(st_ref[2 * h] - tv[2 * h, 0:1, :]) / z
        return carry

    lax.fori_loop(0, PEER_HEADS, finish, 0)


def _topk(st):
    spec = pl.BlockSpec((PEER_HEADS, N_KEYS, TL), lambda i: (0, 0, i))
    spec_g = pl.BlockSpec((PEER_HEADS, N_KEYS // 16, 16, TL), lambda i: (0, 0, 0, i))
    shp = jax.ShapeDtypeStruct((PEER_HEADS, N_KEYS, NT), f32)
    shp_g = jax.ShapeDtypeStruct((PEER_HEADS, N_KEYS // 16, 16, NT), bf16)
    return pl.pallas_call(
        _topk_kernel,
        out_shape=[shp_g, shp_g, shp, shp],
        grid=(NT // TL,),
        in_specs=[pl.BlockSpec((2 * PEER_HEADS, N_KEYS, TL), lambda i: (0, 0, i))],
        out_specs=[spec_g, spec_g, spec, spec],
        scratch_shapes=[pltpu.VMEM((2 * PEER_HEADS, N_KEYS, TL), f32),
                        pltpu.VMEM((2 * PEER_HEADS, TOPK, TL), f32),
                        pltpu.VMEM((2 * PEER_HEADS, TOPK, TL), f32),
                        pltpu.VMEM((PEER_HEADS, N_CAND_ROWS, TL), f32),
                        pltpu.VMEM((PEER_HEADS, N_CAND_ROWS, TL), f32),
                        pltpu.VMEM((PEER_HEADS, N_CAND_ROWS, TL), f32)],
        compiler_params=_cparams(("arbitrary",)), name="peer_topk",
    )(st)


N_PEER_TILES = (NT // PEER_TM) * (N_EXPERTS // PEER_TE)
N_ETILES = N_EXPERTS // PEER_TE


def _peer_kernel(hf_ref, u_ref, vt_ref, rank2_ref, w2_ref, cnt1_ref, w1_ref, o_ref, acc, sbuf, at):
    s = pl.program_id(0)
    slot = s % 2
    last_e = jnp.maximum(s - 2, 0) % N_ETILES

    @pl.when(s == 0)
    def _():
        sbuf[...] = jnp.zeros(sbuf.shape, f32)
        at[...] = jnp.zeros(at.shape, bf16)

    @pl.when(last_e == 0)
    def _():
        acc[...] = jnp.zeros(acc.shape, f32)

    def rows16(ref, h, e):
        return jnp.broadcast_to(ref[h, e:e + 1, :], (16, PEER_TM)).astype(bf16)

    acc[...] += jnp.dot(vt_ref[...], at[1 - slot], preferred_element_type=f32)
    for e in range(ROWS_PER_TE):
        g = jnp.zeros((N_KEYS // 16, 16, PEER_TM), bf16)
        for h in range(PEER_HEADS):
            c = rows16(cnt1_ref, h, e)[None]
            w = rows16(w1_ref, h, e)[None]
            g = g + jnp.where(rank2_ref[h] < c, w2_ref[h] * w, 0.0)
        sv = sbuf[1 - slot, e * N_KEYS:(e + 1) * N_KEYS, :]
        at[slot, e * N_KEYS:(e + 1) * N_KEYS, :] = _gelu_tanh_bf16(sv) * g.reshape(N_KEYS, PEER_TM)
    sbuf[slot] = lax.dot_general(u_ref[...], hf_ref[...], (((1,), (1,)), ((), ())),
                                 preferred_element_type=f32)

    @pl.when((s > 1) & (last_e == N_ETILES - 1))
    def _():
        o_ref[...] = acc[...].T


def _peer(l, hf, u_all, vt_all, rank2, w2, cnt1, w1):
    tile = lambda s, lag: jnp.clip(s - lag, 0, N_PEER_TILES - 1)
    big = pl.BlockSpec((PEER_HEADS, N_KEYS // 16, 16, PEER_TM), lambda s: (0, 0, 0, tile(s, 1) // N_ETILES))
    small = pl.BlockSpec((PEER_HEADS, ROWS_PER_TE, PEER_TM),
                         lambda s: (0, tile(s, 1) % N_ETILES, tile(s, 1) // N_ETILES))
    return pl.pallas_call(
        _peer_kernel,
        out_shape=jax.ShapeDtypeStruct((NT, D), f32),
        grid=(N_PEER_TILES + 2,),
        in_specs=[pl.BlockSpec((PEER_TM, D), lambda s: (tile(s, 0) // N_ETILES, 0)),
                  pl.BlockSpec((None, PEER_TE, D), lambda s: (l, tile(s, 0) % N_ETILES, 0)),
                  pl.BlockSpec((None, D, PEER_TE), lambda s: (l, 0, tile(s, 2) % N_ETILES)),
                  big, big, small, small],
        out_specs=pl.BlockSpec((PEER_TM, D), lambda s: (tile(s, 2) // N_ETILES, 0)),
        scratch_shapes=[pltpu.VMEM((D, PEER_TM), f32), pltpu.VMEM((2, PEER_TE, PEER_TM), f32),
                        pltpu.VMEM((2, PEER_TE, PEER_TM), bf16)],
        compiler_params=_cparams(("arbitrary",)), name="peer_experts",
    )(hf, u_all, vt_all, rank2, w2, cnt1, w1)


def _residual_kernel(x_ref, p_ref, mod_ref, o_ref):
    o_ref[...] = x_ref[...] + mod_ref[5:6, :] * p_ref[...]


def _residual(x1, peer_out, mod_l, blk0, nrows):
    row = lambda i: (blk0 + i, 0)
    return pl.pallas_call(
        _residual_kernel,
        out_shape=jax.ShapeDtypeStruct((nrows, D), f32),
        grid=(nrows // TM,),
        in_specs=[pl.BlockSpec((TM, D), row), pl.BlockSpec((TM, D), row),
                  pl.BlockSpec((None, 6, D), lambda i: (_mod_row(blk0 + i, TM), 0, 0))],
        out_specs=pl.BlockSpec((TM, D), lambda i: (i, 0)),
        compiler_params=_cparams(("arbitrary",)), name="residual",
    )(x1, peer_out, mod_l)


def _rope_tables():
    pos = jnp.arange(T_S)
    row = (pos // GRID_W).astype(f32)
    col = (pos % GRID_W).astype(f32)
    n_freq = QK_ROPE // 4
    inv = 1.0 / (ROPE_THETA ** (jnp.arange(n_freq, dtype=f32) / n_freq))
    ang = jnp.concatenate([row[:, None] * inv, col[:, None] * inv], -1)
    cos, sin = jnp.cos(ang), jnp.sin(ang)
    pad = LANES - QK_ROPE
    cos_t = jnp.concatenate([cos, cos, jnp.ones((T_S, pad), f32)], -1)
    sin_t = jnp.concatenate([-sin, sin, jnp.zeros((T_S, pad), f32)], -1)
    cos_t = jnp.concatenate([cos_t, jnp.ones((TM, LANES), f32)], 0)
    sin_t = jnp.concatenate([sin_t, jnp.zeros((TM, LANES), f32)], 0)
    return cos_t, sin_t


def _block_diag(w):
    eye = jnp.eye(LRU_HEADS, dtype=w.dtype)
    return jnp.einsum('hij,hg->higj', w, eye).reshape(LRU_W, LRU_W)


def kernel(x_prompt, x_sample, cache_ckv, cache_krope, state_lru, c, c_ctx, ada_w, ada_b, norm_mix_g, w_in, mla_qa_g, mla_w_qb, mla_q_g, mla_kva_g, mla_w_kvb, mla_k_g, lru_conv_w, lru_conv_b, lru_w_a, lru_b_a, lru_w_i, lru_b_i, lru_lam, cm_dw_w, cm_dw_b, cm_ln_g, cm_ln_b, grp_g, w_out, norm_ffn_g, peer_w_q, peer_keys, peer_u, peer_v):
    cs = jnp.concatenate([c_ctx[None, :], c, jnp.zeros((8 - 1 - NB_S, D), f32)], 0)
    mod = _modulation(cs, ada_w, ada_b).reshape(L, 8, 6, D)
    cos_t, sin_t = _rope_tables()
    x = jnp.concatenate([x_prompt.reshape(NP, D), x_sample.reshape(NS, D)], 0)
    zeros_h0 = jnp.zeros((NB_P, 2, LRU_W), f32)
    u_all = peer_u.astype(bf16)
    vt_all = jnp.swapaxes(peer_v, 1, 2).astype(bf16)

    peer_out, x1 = None, None
    ckv_list, kr_list, h_list = [], [], []
    for l in range(L):
        wi = w_in[l]
        w_in_p = jnp.concatenate([wi[:, :832], jnp.zeros((D, LANES - QK_ROPE), f32), wi[:, 832:]], 1).astype(bf16)
        w_qb_p = jnp.pad(mla_w_qb[l].reshape(Q_LORA, HEADS, QK_HEAD),
                         ((0, 0), (0, 0), (0, HEAD_PAD - QK_HEAD))).reshape(Q_LORA, HEADS * HEAD_PAD).astype(bf16)
        qg_p = jnp.tile(jnp.pad(mla_q_g[l], (0, HEAD_PAD - QK_HEAD)), HEADS)[None, :]
        kgn = mla_k_g[l][None, :QK_NOPE]
        kgr = jnp.pad(mla_k_g[l][QK_NOPE:], (0, LANES - QK_ROPE))[None, :]
        wg = jnp.concatenate([_block_diag(lru_w_a[l, 0]), _block_diag(lru_w_a[l, 1]),
                              _block_diag(lru_w_i[l, 0]), _block_diag(lru_w_i[l, 1])], 1).astype(bf16)
        bg = jnp.concatenate([lru_b_a[l, 0], lru_b_a[l, 1], lru_b_i[l, 0], lru_b_i[l, 1]])[None, :]
        g_att = grp_g[l][None, :1024]
        g_lru = grp_g[l][None, 1024:1536]
        g_conv = grp_g[l][None, 1536:]
        seq_w = [lru_conv_w[l], lru_conv_b[l][None, :], wg, bg, lru_lam[l], g_lru,
                 cm_dw_w[l], cm_dw_b[l][None, :], cm_ln_g[l][None, :], cm_ln_b[l][None, :], g_conv]

        x, (qn, ckv, zkr, zx, zg, zc) = _inproj(
            x1 if l else x, peer_out, mod[l - 1] if l else None, mod[l],
            norm_mix_g[l][None, :], w_in_p, mla_qa_g[l][None, :], mla_kva_g[l][None, :])
        ckv_list.append(ckv[:NP].reshape(NB_P, T_P, KV_LORA))
        kr_list.append(zkr[:NP, :QK_ROPE].reshape(NB_P, T_P, QK_ROPE))

        q = _qproj(qn, w_qb_p, qg_p, cos_t, sin_t)
        ckv_all = jnp.concatenate(
            [jnp.concatenate([cache_ckv[:, l], ckv[NP:].reshape(NB_S, T_S, KV_LORA)], 1).reshape(NB_S * T_KV, KV_LORA),
             ckv[:NP]], 0)
        kr_ctx = jnp.pad(cache_krope[:, l], ((0, 0), (0, 0), (0, LANES - QK_ROPE)))
        kr_all = jnp.concatenate(
            [jnp.concatenate([kr_ctx, zkr[NP:].reshape(NB_S, T_S, LANES)], 1).reshape(NB_S * T_KV, LANES),
             zkr[:NP]], 0)
        k, v = _kvproj(ckv_all, kr_all, mla_w_kvb[l].astype(bf16), kgn, kgr, cos_t, sin_t)

        attn_p = _attention_ctx(q, k, v)
        attn_s = _attention_lat(q, k, v)

        lru_p, conv_p, hfin = _seqmix(zx, zg, zc, zeros_h0, NB_P, T_P, 0, seq_w)
        lru_s, conv_s, _ = _seqmix(zx, zg, zc, state_lru[:, l], NB_S, T_S, NP // T_S, seq_w)
        h_list.append(hfin)

        x1, hf = _outproj(attn_p, attn_s, lru_p, lru_s, conv_p, conv_s, x, mod[l], g_att,
                          w_out[l].astype(bf16), norm_ffn_g[l][None, :])
        st = _peerq(hf, peer_w_q[l].astype(bf16),
                    peer_keys[l].reshape(2 * PEER_HEADS, N_KEYS, N_KEYS).astype(bf16))
        rank2, w2, cnt1, w1 = _topk(st)
        peer_out = _peer(l, hf, u_all, vt_all, rank2, w2, cnt1, w1)

    y_p = _residual(x1, peer_out, mod[L - 1], 0, NP)
    y_s = _residual(x1, peer_out, mod[L - 1], NP // TM, NS)
    new_ckv = jnp.stack(ckv_list, 1)
    new_krope = jnp.stack(kr_list, 1)
    new_lru = jnp.stack(h_list, 1)
    return (y_p.reshape(NB_P, T_P, D), y_s.reshape(NB_S, T_S, D), new_ckv, new_krope, new_lru)
```

```python
import functools

import jax
import jax.numpy as jnp
from jax import lax
from jax.experimental import pallas as pl
from jax.experimental.pallas import tpu as pltpu

f32 = jnp.float32
bf16 = jnp.bfloat16
i32 = jnp.int32

D = 2048
NB_P, T_P = 16, 256
NB_S, T_S = 4, 1024
T_CTX = 512
NP = NB_P * T_P
NS = NB_S * T_S
NT = NP + NS
T_KV = T_CTX + T_S
NKV = NB_S * T_KV + NP
L = 2
HEADS = 8
QK_NOPE, QK_ROPE, V_HEAD = 128, 64, 128
QK_HEAD = QK_NOPE + QK_ROPE
Q_LORA, KV_LORA = 512, 256
LRU_W, CONV_W = 512, 512
LRU_HEADS, LRU_HD = 8, 64
LRU_K, CM_K = 4, 31
RG_C = 8.0
PEER_HEADS, N_KEYS, TOPK = 8, 128, 16
N_EXPERTS = N_KEYS * N_KEYS
EPS = 1e-6
LN_EPS = 1e-5
ROPE_THETA = 10000.0
GRID_W = 64

LANES = 128
HEAD_PAD = 256
IN_W_PAD = 2944
TM = 512
TQ_S = 512
TL = 128
SEQ_CHUNK = 64
PEER_TM = 512
PEER_TE = 1024
ROWS_PER_TE = PEER_TE // N_KEYS
VMEM_LIMIT = 56 * 1024 * 1024


def _cparams(sem):
    return pltpu.CompilerParams(dimension_semantics=sem, vmem_limit_bytes=VMEM_LIMIT)


def _mod_row(i, tm):
    return jnp.where(i < NP // tm, 0, 1 + (i * tm - NP) // T_S)


def _rms(x, g):
    return (x * lax.rsqrt(jnp.mean(x * x, -1, keepdims=True) + EPS)) * g


def _gelu_tanh(x):
    return 0.5 * x * (1.0 + jnp.tanh(0.7978845608028654 * (x + 0.044715 * (x * x * x))))


def _sigmoid(x):
    return 1.0 / (1.0 + jnp.exp(-x))


def _rope(a, cos, sin):
    lane = lax.broadcasted_iota(i32, a.shape, 1)
    half = QK_ROPE // 2
    sw = jnp.where(lane < half, pltpu.roll(a, LANES - half, 1), pltpu.roll(a, half, 1))
    return a * cos + sw * sin


def _mod_kernel(c_ref, w_ref, b_ref, o_ref):
    c = c_ref[...]
    a = (c * _sigmoid(c)).astype(bf16)
    o_ref[...] = jnp.dot(a, w_ref[...].astype(bf16), preferred_element_type=f32) + b_ref[...]


def _modulation(cs, ada_w, ada_b):
    tn = 1536
    return pl.pallas_call(
        _mod_kernel,
        out_shape=jax.ShapeDtypeStruct((L, 8, 6 * D), f32),
        grid=(L, 6 * D // tn),
        in_specs=[pl.BlockSpec((8, D), lambda l, j: (0, 0)),
                  pl.BlockSpec((None, D, tn), lambda l, j: (l, 0, j)),
                  pl.BlockSpec((None, 1, tn), lambda l, j: (l, 0, j))],
        out_specs=pl.BlockSpec((None, 8, tn), lambda l, j: (l, 0, j)),
        compiler_params=_cparams(("arbitrary", "arbitrary")),
        name="ada_mod",
    )(cs, ada_w, ada_b.reshape(L, 1, 6 * D))


def _inproj_kernel(has_res, *refs):
    if has_res:
        (x_ref, p_ref, mprev_ref, mod_ref, ng_ref, w_ref, qag_ref, kvag_ref,
         xo_ref, qn_ref, ckv_ref, zkr_ref, zx_ref, zg_ref, zc_ref) = refs
        x = x_ref[...] + mprev_ref[5:6, :] * p_ref[...]
        xo_ref[...] = x
    else:
        (x_ref, mod_ref, ng_ref, w_ref, qag_ref, kvag_ref,
         qn_ref, ckv_ref, zkr_ref, zx_ref, zg_ref, zc_ref) = refs
        x = x_ref[...]
    m = mod_ref[...]
    hn = _rms(x, ng_ref[...]) * (1.0 + m[1:2, :]) + m[0:1, :]
    hb = hn.astype(bf16)

    def seg(a, b):
        return jnp.dot(hb, w_ref[:, a:b], preferred_element_type=f32)

    qn_ref[...] = _rms(seg(0, 512), qag_ref[...]).astype(bf16)
    ckv_ref[...] = _rms(seg(512, 768), kvag_ref[...])
    zkr_ref[...] = seg(768, 896)
    zx_ref[...] = seg(896, 1408)
    zg_ref[...] = seg(1408, 1920)
    zc_ref[...] = seg(1920, 2944)


def _inproj(x, peer_out, mod_prev, mod_l, ng, w_in_p, qag, kvag):
    has_res = peer_out is not None
    row = lambda i: (i, 0)
    modspec = pl.BlockSpec((None, 6, D), lambda i: (_mod_row(i, TM), 0, 0))
    full = lambda shape: pl.BlockSpec(shape, lambda i: (0,) * len(shape))
    in_specs = [pl.BlockSpec((TM, D), row)]
    args = [x]
    if has_res:
        in_specs += [pl.BlockSpec((TM, D), row), modspec]
        args += [peer_out, mod_prev]
    in_specs += [modspec, full((1, D)), full((D, IN_W_PAD)), full((1, Q_LORA)), full((1, KV_LORA))]
    args += [mod_l, ng, w_in_p, qag, kvag]
    out_shape = [jax.ShapeDtypeStruct((NT, Q_LORA), bf16),
                 jax.ShapeDtypeStruct((NT, KV_LORA), f32),
                 jax.ShapeDtypeStruct((NT, LANES), f32),
                 jax.ShapeDtypeStruct((NT, LRU_W), f32),
                 jax.ShapeDtypeStruct((NT, LRU_W), f32),
                 jax.ShapeDtypeStruct((NT, 2 * CONV_W), f32)]
    out_specs = [pl.BlockSpec((TM, Q_LORA), row), pl.BlockSpec((TM, KV_LORA), row),
                 pl.BlockSpec((TM, LANES), row), pl.BlockSpec((TM, LRU_W), row),
                 pl.BlockSpec((TM, LRU_W), row), pl.BlockSpec((TM, 2 * CONV_W), row)]
    if has_res:
        out_shape = [jax.ShapeDtypeStruct((NT, D), f32)] + out_shape
        out_specs = [pl.BlockSpec((TM, D), row)] + out_specs
    outs = pl.pallas_call(
        functools.partial(_inproj_kernel, has_res),
        out_shape=out_shape, grid=(NT // TM,), in_specs=in_specs, out_specs=out_specs,
        compiler_params=_cparams(("arbitrary",)), name="in_proj",
    )(*args)
    if has_res:
        return outs[0], outs[1:]
    return x, outs


def _qproj_kernel(qn_ref, w_ref, g_ref, cos_ref, sin_ref, q_ref):
    qn = qn_ref[...]
    cos, sin = cos_ref[...], sin_ref[...]
    scale = QK_HEAD ** -0.5
    for h in range(HEADS):
        lo = h * HEAD_PAD
        qh = jnp.dot(qn, w_ref[:, lo:lo + HEAD_PAD], preferred_element_type=f32)
        r = lax.rsqrt(jnp.sum(qh * qh, -1, keepdims=True) * (1.0 / QK_HEAD) + EPS)
        qh = (qh * r) * g_ref[:, lo:lo + HEAD_PAD]
        q_ref[:, lo:lo + LANES] = (qh[:, :LANES] * scale).astype(bf16)
        q_ref[:, lo + LANES:lo + HEAD_PAD] = (_rope(qh[:, LANES:], cos, sin) * scale).astype(bf16)


def _qproj(qn, w_qb_p, qg_p, cos_t, sin_t):
    nblk_p = NP // TM
    tab = lambda i: (jnp.where(i < nblk_p, T_S // TM, (i - nblk_p) % (T_S // TM)), 0)
    return pl.pallas_call(
        _qproj_kernel,
        out_shape=jax.ShapeDtypeStruct((NT, HEADS * HEAD_PAD), bf16),
        grid=(NT // TM,),
        in_specs=[pl.BlockSpec((TM, Q_LORA), lambda i: (i, 0)),
                  pl.BlockSpec((Q_LORA, HEADS * HEAD_PAD), lambda i: (0, 0)),
                  pl.BlockSpec((1, HEADS * HEAD_PAD), lambda i: (0, 0)),
                  pl.BlockSpec((TM, LANES), tab), pl.BlockSpec((TM, LANES), tab)],
        out_specs=pl.BlockSpec((TM, HEADS * HEAD_PAD), lambda i: (i, 0)),
        compiler_params=_cparams(("arbitrary",)), name="q_proj",
    )(qn, w_qb_p, qg_p, cos_t, sin_t)


def _kvproj_kernel(ckv_ref, kr_ref, w_ref, gn_ref, gr_ref, cos_ref, sin_ref, k_ref, v_ref):
    ckv = ckv_ref[...].astype(bf16)
    kr = kr_ref[...]
    ssr = jnp.sum(kr * kr, -1, keepdims=True)
    cos, sin = cos_ref[...], sin_ref[...]
    for h in range(HEADS):
        lo = h * HEAD_PAD
        kv = jnp.dot(ckv, w_ref[:, lo:lo + HEAD_PAD], preferred_element_type=f32)
        kn = kv[:, :QK_NOPE]
        r = lax.rsqrt((jnp.sum(kn * kn, -1, keepdims=True) + ssr) * (1.0 / QK_HEAD) + EPS)
        k_ref[:, lo:lo + LANES] = ((kn * r) * gn_ref[...]).astype(bf16)
        k_ref[:, lo + LANES:lo + HEAD_PAD] = _rope((kr * r) * gr_ref[...], cos, sin).astype(bf16)
        v_ref[:, h * V_HEAD:(h + 1) * V_HEAD] = kv[:, QK_NOPE:].astype(bf16)


def _kvproj(ckv_all, kr_all, w_kvb, kgn, kgr, cos_t, sin_t):
    nblk_s = NB_S * T_KV // TM
    per = T_KV // TM
    ident = T_S // TM
    tab = lambda i: (jnp.where(i < nblk_s, (i % per + ident) % per, ident), 0)
    return pl.pallas_call(
        _kvproj_kernel,
        out_shape=[jax.ShapeDtypeStruct((NKV, HEADS * HEAD_PAD), bf16),
                   jax.ShapeDtypeStruct((NKV, HEADS * V_HEAD), bf16)],
        grid=(NKV // TM,),
        in_specs=[pl.BlockSpec((TM, KV_LORA), lambda i: (i, 0)),
                  pl.BlockSpec((TM, LANES), lambda i: (i, 0)),
                  pl.BlockSpec((KV_LORA, HEADS * HEAD_PAD), lambda i: (0, 0)),
                  pl.BlockSpec((1, LANES), lambda i: (0, 0)),
                  pl.BlockSpec((1, LANES), lambda i: (0, 0)),
                  pl.BlockSpec((TM, LANES), tab), pl.BlockSpec((TM, LANES), tab)],
        out_specs=[pl.BlockSpec((TM, HEADS * HEAD_PAD), lambda i: (i, 0)),
                   pl.BlockSpec((TM, HEADS * V_HEAD), lambda i: (i, 0))],
        compiler_params=_cparams(("arbitrary",)), name="kv_proj",
    )(ckv_all, kr_all, w_kvb, kgn, kgr, cos_t, sin_t)


def _attn_head(q, k, v):
    s = lax.dot_general(q, k, (((1,), (1,)), ((), ())), preferred_element_type=f32)
    p = jnp.exp(s - jnp.max(s, -1, keepdims=True))
    o = jnp.dot(p.astype(bf16), v, preferred_element_type=f32)
    return o / jnp.sum(p, -1, keepdims=True)


def _attn_kernel(q_ref, k_ref, v_ref, o_ref):
    o_ref[...] = _attn_head(q_ref[...], k_ref[...], v_ref[...])


def _attn_allheads_kernel(q_ref, k_ref, v_ref, o_ref):
    for h in range(HEADS):
        qk = slice(h * HEAD_PAD, (h + 1) * HEAD_PAD)
        hv = slice(h * V_HEAD, (h + 1) * V_HEAD)
        o_ref[:, hv] = _attn_head(q_ref[:, qk], k_ref[:, qk], v_ref[:, hv])


def _attention_ctx(q, k, v):
    kv0 = NB_S * T_KV // T_P
    return pl.pallas_call(
        _attn_allheads_kernel,
        out_shape=jax.ShapeDtypeStruct((NP, HEADS * V_HEAD), f32),
        grid=(NB_P,),
        in_specs=[pl.BlockSpec((T_P, HEADS * HEAD_PAD), lambda b: (b, 0)),
                  pl.BlockSpec((T_P, HEADS * HEAD_PAD), lambda b: (kv0 + b, 0)),
                  pl.BlockSpec((T_P, HEADS * V_HEAD), lambda b: (kv0 + b, 0))],
        out_specs=pl.BlockSpec((T_P, HEADS * V_HEAD), lambda b: (b, 0)),
        compiler_params=_cparams(("arbitrary",)), name="attention_ctx",
    )(q, k, v)


def _attention_lat(q, k, v):
    nq = T_S // TQ_S
    q0 = NP // TQ_S
    return pl.pallas_call(
        _attn_kernel,
        out_shape=jax.ShapeDtypeStruct((NS, HEADS * V_HEAD), f32),
        grid=(NB_S, HEADS, nq),
        in_specs=[pl.BlockSpec((TQ_S, HEAD_PAD), lambda b, h, i: (q0 + b * nq + i, h)),
                  pl.BlockSpec((T_KV, HEAD_PAD), lambda b, h, i: (b, h)),
                  pl.BlockSpec((T_KV, V_HEAD), lambda b, h, i: (b, h))],
        out_specs=pl.BlockSpec((TQ_S, V_HEAD), lambda b, h, i: (b * nq + i, h)),
        compiler_params=_cparams(("arbitrary", "arbitrary", "arbitrary")), name="attention_lat",
    )(q, k, v)


def _seqmix_kernel(T, zx_ref, zg_ref, zc_ref, h0_ref, lcw_ref, lcb_ref, wg_ref, bg_ref, lam_ref,
                   glru_ref, cw_ref, cb_ref, lng_ref, lnb_ref, gconv_ref,
                   lru_ref, conv_ref, hfin_ref, xpad, a_s, u_s, hs, cpad, xc_s, g_s, wbuf):
    n_chunks = T // SEQ_CHUNK

    def chunked(body, static=False):
        if static:
            for c in range(n_chunks):
                body(c * SEQ_CHUNK)
            return

        def step(c, carry):
            body(pl.multiple_of(c * SEQ_CHUNK, SEQ_CHUNK))
            return carry
        lax.fori_loop(0, n_chunks, step, 0)

    xpad[0:8, :] = jnp.zeros((8, LRU_W), f32)
    xpad[8:8 + T, :] = zx_ref[...]
    xpad[8 + T:16 + T, :] = jnp.zeros((8, LRU_W), f32)

    def lru_conv(r0):
        xc = lcb_ref[...] + lcw_ref[0:1, :] * xpad[pl.ds(r0 + 6, SEQ_CHUNK), :]
        for k in range(1, LRU_K):
            xc = xc + lcw_ref[k:k + 1, :] * xpad[pl.ds(r0 + 6 + k, SEQ_CHUNK), :]
        xc_s[pl.ds(r0, SEQ_CHUNK), :] = xc

    chunked(lru_conv, static=True)
    xcb = xc_s[...].astype(bf16)
    for j in range(4):
        cols = slice(j * LRU_W, (j + 1) * LRU_W)
        g_s[:, cols] = jnp.dot(xcb, wg_ref[:, cols], preferred_element_type=f32) + bg_ref[:, cols]
    lam = lam_ref[...]
    sp = jnp.maximum(-lam, 0.0) + jnp.log(1.0 + jnp.exp(-jnp.abs(lam)))

    def lru_coeffs(r0):
        rows = pl.ds(r0, SEQ_CHUNK)
        xc = xc_s[rows, :]
        for d in range(2):
            r = _sigmoid(g_s[rows, d * LRU_W:(d + 1) * LRU_W])
            gi = _sigmoid(g_s[rows, (2 + d) * LRU_W:(3 + d) * LRU_W])
            a = jnp.exp((-RG_C * r) * sp[d:d + 1, :])
            a_s[d, rows, :] = a
            u_s[d, rows, :] = (jnp.sqrt(1.0 - a * a) * gi) * xc

    chunked(lru_coeffs)
    h0 = h0_ref[...]

    rowi = lax.broadcasted_iota(i32, (8, LRU_W), 0)

    def tile_scan(a, u, reverse):
        for d in (1, 2, 4):
            if reverse:
                keep = rowi < 8 - d
                sa, su = pltpu.roll(a, 8 - d, 0), pltpu.roll(u, 8 - d, 0)
            else:
                keep = rowi >= d
                sa, su = pltpu.roll(a, d, 0), pltpu.roll(u, d, 0)
            u = u + a * jnp.where(keep, su, 0.0)
            a = a * jnp.where(keep, sa, 1.0)
        return a, u

    def blk(i, carry):
        hf, hb = carry
        r0 = pl.multiple_of(i * 8, 8)
        ca, cu = tile_scan(a_s[0, pl.ds(r0, 8), :], u_s[0, pl.ds(r0, 8), :], False)
        rows = ca * hf + cu
        hs[0, pl.ds(r0, 8), :] = rows
        hf = rows[7:8, :]
        r1 = pl.multiple_of(T - 8 - i * 8, 8)
        ca, cu = tile_scan(a_s[1, pl.ds(r1, 8), :], u_s[1, pl.ds(r1, 8), :], True)
        rows = ca * hb + cu
        hs[1, pl.ds(r1, 8), :] = rows
        hb = rows[0:1, :]
        return hf, hb

    hf, hb = lax.fori_loop(0, T // 8, blk, (h0[0:1, :], h0[1:2, :]))
    hfin_ref[0:1, :] = hf
    hfin_ref[1:2, :] = hb
    def lru_out(r0):
        rows = pl.ds(r0, SEQ_CHUNK)
        y = (hs[0, rows, :] + hs[1, rows, :]) * _gelu_tanh(zg_ref[rows, :])
        lru_ref[rows, :] = _rms(y, glru_ref[...]).astype(bf16)

    chunked(lru_out)

    cpad[0:16, :] = jnp.zeros((16, CONV_W), f32)
    cpad[16 + T:32 + T, :] = jnp.zeros((16, CONV_W), f32)

    def glu(r0):
        rows = pl.ds(r0, SEQ_CHUNK)
        cpad[pl.ds(r0 + 16, SEQ_CHUNK), :] = zc_ref[rows, 0:CONV_W] * _sigmoid(zc_ref[rows, CONV_W:2 * CONV_W])

    chunked(glu)

    def conv_out(r0):
        acc = cb_ref[...]
        for res in range(8):
            taps = [k for k in range(CM_K) if (1 + k) % 8 == res]
            span = max(1 + k - res for k in taps) + SEQ_CHUNK
            wbuf[res, 0:span, :] = cpad[pl.ds(r0 + res, span), :]
            for k in taps:
                off = 1 + k - res
                acc = acc + cw_ref[k:k + 1, :] * wbuf[res, off:off + SEQ_CHUNK, :]
        mu = jnp.mean(acc, -1, keepdims=True)
        cen = acc - mu
        var = jnp.mean(cen * cen, -1, keepdims=True)
        hln = (cen * lax.rsqrt(var + LN_EPS)) * lng_ref[...] + lnb_ref[...]
        hsl = hln * _sigmoid(hln)
        conv_ref[pl.ds(r0, SEQ_CHUNK), :] = _rms(hsl, gconv_ref[...]).astype(bf16)

    chunked(conv_out, static=True)


def _seqmix(zx, zg, zc, h0, nb, T, blk0, wts):
    seq = lambda w: pl.BlockSpec((T, w), lambda b: (blk0 + b, 0))
    full = lambda a: pl.BlockSpec(a.shape, lambda b: (0,) * a.ndim)
    return pl.pallas_call(
        functools.partial(_seqmix_kernel, T),
        out_shape=[jax.ShapeDtypeStruct((nb * T, LRU_W), bf16),
                   jax.ShapeDtypeStruct((nb * T, CONV_W), bf16),
                   jax.ShapeDtypeStruct((nb, 2, LRU_W), f32)],
        grid=(nb,),
        in_specs=[seq(LRU_W), seq(LRU_W), seq(2 * CONV_W),
                  pl.BlockSpec((None, 2, LRU_W), lambda b: (b, 0, 0))] + [full(a) for a in wts],
        out_specs=[pl.BlockSpec((T, LRU_W), lambda b: (b, 0)),
                   pl.BlockSpec((T, CONV_W), lambda b: (b, 0)),
                   pl.BlockSpec((None, 2, LRU_W), lambda b: (b, 0, 0))],
        scratch_shapes=[pltpu.VMEM((T + 16, LRU_W), f32), pltpu.VMEM((2, T, LRU_W), f32),
                        pltpu.VMEM((2, T, LRU_W), f32), pltpu.VMEM((2, T, LRU_W), f32),
                        pltpu.VMEM((T + 32, CONV_W), f32), pltpu.VMEM((T, LRU_W), f32),
                        pltpu.VMEM((T, 4 * LRU_W), f32),
                        pltpu.VMEM((8, SEQ_CHUNK + 24, CONV_W), f32)],
        compiler_params=_cparams(("arbitrary",)), name="seq_mix",
    )(zx, zg, zc, h0, *wts)


def _outproj_kernel(attn_p_ref, attn_s_ref, lru_p_ref, lru_s_ref, conv_p_ref, conv_s_ref,
                    x_ref, mod_ref, gatt_ref, wo_ref, nf_ref, x1_ref, hf_ref):
    is_ctx = pl.program_id(0) < NP // TM
    attn = jnp.where(is_ctx, attn_p_ref[...], attn_s_ref[...])
    lru = jnp.where(is_ctx, lru_p_ref[...], lru_s_ref[...])
    conv = jnp.where(is_ctx, conv_p_ref[...], conv_s_ref[...])
    an = _rms(attn, gatt_ref[...]).astype(bf16)
    mixed = jnp.dot(an, wo_ref[0:1024, :], preferred_element_type=f32)
    mixed = mixed + jnp.dot(lru, wo_ref[1024:1536, :], preferred_element_type=f32)
    mixed = mixed + jnp.dot(conv, wo_ref[1536:2048, :], preferred_element_type=f32)
    m = mod_ref[...]
    x1 = x_ref[...] + m[2:3, :] * mixed
    x1_ref[...] = x1
    hf_ref[...] = (_rms(x1, nf_ref[...]) * (1.0 + m[4:5, :]) + m[3:4, :]).astype(bf16)


def _outproj(attn_p, attn_s, lru_p, lru_s, conv_p, conv_s, x, mod_l, gatt, w_out, nf):
    row = lambda i: (i, 0)
    nblk_p = NP // TM
    ctx = lambda i: (jnp.minimum(i, nblk_p - 1), 0)
    lat = lambda i: (jnp.maximum(i - nblk_p, 0), 0)
    return pl.pallas_call(
        _outproj_kernel,
        out_shape=[jax.ShapeDtypeStruct((NT, D), f32), jax.ShapeDtypeStruct((NT, D), bf16)],
        grid=(NT // TM,),
        in_specs=[pl.BlockSpec((TM, HEADS * V_HEAD), ctx), pl.BlockSpec((TM, HEADS * V_HEAD), lat),
                  pl.BlockSpec((TM, LRU_W), ctx), pl.BlockSpec((TM, LRU_W), lat),
                  pl.BlockSpec((TM, CONV_W), ctx), pl.BlockSpec((TM, CONV_W), lat),
                  pl.BlockSpec((TM, D), row),
                  pl.BlockSpec((None, 6, D), lambda i: (_mod_row(i, TM), 0, 0)),
                  pl.BlockSpec((1, HEADS * V_HEAD), lambda i: (0, 0)),
                  pl.BlockSpec((D, D), lambda i: (0, 0)),
                  pl.BlockSpec((1, D), lambda i: (0, 0))],
        out_specs=[pl.BlockSpec((TM, D), row), pl.BlockSpec((TM, D), row)],
        compiler_params=_cparams(("arbitrary",)), name="out_proj",
    )(attn_p, attn_s, lru_p, lru_s, conv_p, conv_s, x, mod_l, gatt, w_out, nf)


def _peerq_kernel(hf_ref, wq_ref, keys_ref, st_ref):
    q = jnp.dot(hf_ref[...], wq_ref[...], preferred_element_type=f32).astype(bf16)
    for hp in range(2 * PEER_HEADS):
        st_ref[hp] = lax.dot_general(keys_ref[hp], q[:, hp * N_KEYS:(hp + 1) * N_KEYS],
                                     (((1,), (1,)), ((), ())), preferred_element_type=f32)


def _peerq(hf, wq, keys):
    return pl.pallas_call(
        _peerq_kernel,
        out_shape=jax.ShapeDtypeStruct((2 * PEER_HEADS, N_KEYS, NT), f32),
        grid=(NT // TM,),
        in_specs=[pl.BlockSpec((TM, D), lambda i: (i, 0)),
                  pl.BlockSpec((D, D), lambda i: (0, 0)),
                  pl.BlockSpec((2 * PEER_HEADS, N_KEYS, N_KEYS), lambda i: (0, 0, 0))],
        out_specs=pl.BlockSpec((2 * PEER_HEADS, N_KEYS, TM), lambda i: (0, 0, i)),
        compiler_params=_cparams(("arbitrary",)), name="peer_query",
    )(hf, wq, keys)


_CAND_GROUPS = (((0, 8),), ((0, 8),), ((1, 8),), ((2, 5), (4, 3)), ((3, 4), (5, 2), (6, 2)), ((7, 2),), None)
N_CAND_ROWS = 8 * len(_CAND_GROUPS)


def _pick(sub, parts, fill):
    out = fill
    for start, val in parts:
        out = jnp.where(sub >= start, val, out)
    return out


def _topk_kernel(st_ref, rank2_ref, w2_ref, cnt1_ref, w1_ref, xs, tv, ti, cs, es, ss):
    n_arr = 2 * PEER_HEADS
    ninf = -jnp.inf
    kio = lax.broadcasted_iota(i32, (N_KEYS, TL), 0).astype(f32)
    sub = lax.broadcasted_iota(i32, (8, TL), 0)
    subf = sub.astype(f32)

    xs[...] = st_ref[...]

    def first(r, carry):
        for a in range(n_arr):
            x = xs[a]
            m = jnp.max(x, axis=0, keepdims=True)
            idx = jnp.min(jnp.where(x == m, kio, float(N_KEYS)), axis=0, keepdims=True)
            xs[a] = jnp.where(kio == idx, ninf, x)
            tv[a, pl.ds(r, 1), :] = m
            ti[a, pl.ds(r, 1), :] = idx
        return carry

    lax.fori_loop(0, TOPK, first, 0)

    pos_groups = []
    for gi, grp in enumerate(_CAND_GROUPS):
        if grp is None:
            pos_groups.append((subf + 8.0) * TOPK)
            continue
        parts, start = [], 0
        for r1, n in grp:
            r2_0 = 8.0 if gi == 1 else 0.0
            parts.append((start, r1 * TOPK + r2_0 + (subf - start)))
            start += n
        pos_groups.append(_pick(sub, parts[1:], parts[0][1]) if start == 8
                          else _pick(sub, [(start, 1e4)], _pick(sub, parts[1:], parts[0][1])))
    pos = jnp.concatenate(pos_groups, axis=0)

    for h in range(PEER_HEADS):
        t1 = tv[2 * h]
        t2 = tv[2 * h + 1]
        lo, hi = t2[0:8, :], t2[8:16, :]
        groups = []
        for gi, grp in enumerate(_CAND_GROUPS):
            if grp is None:
                groups.append(t1[8:16, :] + lo[0:1, :])
                continue
            src = hi if gi == 1 else lo
            a_parts, b_parts, start = [], [], 0
            for r1, n in grp:
                a_parts.append((start, t1[r1:r1 + 1, :]))
                b_parts.append((start, src if start == 0 else pltpu.roll(src, start, 0)))
                start += n
            c = _pick(sub, a_parts[1:], a_parts[0][1]) + _pick(sub, b_parts[1:], b_parts[0][1])
            groups.append(c if start == 8 else jnp.where(sub >= start, ninf, c))
        cand = jnp.concatenate(groups, axis=0)
        cs[h] = cand
        es[h] = jnp.exp(cand - cand[0:1, :])
    ss[...] = jnp.zeros(ss.shape, f32)

    def second(_, carry):
        for h in range(PEER_HEADS):
            cv = cs[h]
            m = jnp.max(cv, axis=0, keepdims=True)
            p = jnp.min(jnp.where(cv == m, pos, 1e5), axis=0, keepdims=True)
            hit = pos == p
            cs[h] = jnp.where(hit, ninf, cv)
            ss[h] = jnp.where(hit, 1.0, ss[h])
        return carry

    lax.fori_loop(0, TOPK, second, 0)

    def colsum(v):
        return jnp.sum(v, axis=0, keepdims=True)

    def finish(h, carry):
        sel = ss[h]
        z = colsum(sel * es[h])
        cnt = [None] * TOPK
        for gi, grp in enumerate(_CAND_GROUPS):
            sg = sel[8 * gi:8 * gi + 8, :]
            if grp is None:
                for i in range(8):
                    cnt[8 + i] = sg[i:i + 1, :]
                continue
            start = 0
            for r1, n in grp:
                part = colsum(jnp.where(sub >= start, jnp.where(sub < start + n, sg, 0.0), 0.0))
                cnt[r1] = part if cnt[r1] is None else cnt[r1] + part
                start += n
        i1 = ti[2 * h]
        i2 = ti[2 * h + 1]
        cnt1 = jnp.zeros((N_KEYS, TL), f32)
        rank2 = jnp.full((N_KEYS, TL), float(TOPK), f32)
        for r in range(TOPK):
            cnt1 = jnp.where(kio == i1[r:r + 1, :], cnt[r], cnt1)
            rank2 = jnp.where(kio == i2[r:r + 1, :], float(r), rank2)
        def grouped(v):
            return v.reshape(N_KEYS // 16, 16, TL).astype(bf16)

        rank2_ref[h] = grouped(rank2)
        cnt1_ref[h] = cnt1
        w2_ref[h] = grouped(jnp.exp(st_ref[2 * h + 1] - tv[2 * h + 1, 0:1, :]))
        w1_ref[h] = jnp.exp(st_ref[2 * h] - tv[2 * h, 0:1, :]) / z
        return carry

    lax.fori_loop(0, PEER_HEADS, finish, 0)


def _topk(st):
    spec = pl.BlockSpec((PEER_HEADS, N_KEYS, TL), lambda i: (0, 0, i))
    spec_g = pl.BlockSpec((PEER_HEADS, N_KEYS // 16, 16, TL), lambda i: (0, 0, 0, i))
    shp = jax.ShapeDtypeStruct((PEER_HEADS, N_KEYS, NT), f32)
    shp_g = jax.ShapeDtypeStruct((PEER_HEADS, N_KEYS // 16, 16, NT), bf16)
    return pl.pallas_call(
        _topk_kernel,
        out_shape=[shp_g, shp_g, shp, shp],
        grid=(NT // TL,),
        in_specs=[pl.BlockSpec((2 * PEER_HEADS, N_KEYS, TL), lambda i: (0, 0, i))],
        out_specs=[spec_g, spec_g, spec, spec],
        scratch_shapes=[pltpu.VMEM((2 * PEER_HEADS, N_KEYS, TL), f32),
                        pltpu.VMEM((2 * PEER_HEADS, TOPK, TL), f32),
                        pltpu.VMEM((2 * PEER_HEADS, TOPK, TL), f32),
                        pltpu.VMEM((PEER_HEADS, N_CAND_ROWS, TL), f32),
                        pltpu.VMEM((PEER_HEADS, N_CAND_ROWS, TL), f32),
                        pltpu.VMEM((PEER_HEADS, N_CAND_ROWS, TL), f32)],
        compiler_params=_cparams(("arbitrary",)), name="peer_topk",
    )(st)


N_PEER_TILES = (NT // PEER_TM) * (N_EXPERTS // PEER_TE)
N_ETILES = N_EXPERTS // PEER_TE


def _peer_kernel(hf_ref, u_ref, vt_ref, rank2_ref, w2_ref, cnt1_ref, w1_ref, gc_ref, o_ref, acc, sbuf, at):
    s = pl.program_id(0)
    slot = s % 2
    last_e = jnp.maximum(s - 2, 0) % N_ETILES

    @pl.when(s == 0)
    def _():
        sbuf[...] = jnp.zeros(sbuf.shape, bf16)
        at[...] = jnp.zeros(at.shape, bf16)

    @pl.when(last_e == 0)
    def _():
        acc[...] = jnp.zeros(acc.shape, f32)

    def rows16(ref, h, e):
        return jnp.broadcast_to(ref[h, e:e + 1, :], (16, PEER_TM)).astype(bf16)

    k1 = jnp.broadcast_to(gc_ref[0:1, :], (16, PEER_TM)).astype(bf16)[None]
    k3 = jnp.broadcast_to(gc_ref[1:2, :], (16, PEER_TM)).astype(bf16)[None]
    acc[...] += jnp.dot(vt_ref[...], at[1 - slot], preferred_element_type=f32)
    for e in range(ROWS_PER_TE):
        g = jnp.zeros((N_KEYS // 16, 16, PEER_TM), bf16)
        for h in range(PEER_HEADS):
            c = rows16(cnt1_ref, h, e)[None]
            w = rows16(w1_ref, h, e)[None]
            g = g + jnp.where(rank2_ref[h] < c, w2_ref[h] * w, 0.0)
        sv = sbuf[1 - slot, e * N_KEYS:(e + 1) * N_KEYS, :].reshape(N_KEYS // 16, 16, PEER_TM)
        half = sv * 0.5
        act = half + half * jnp.tanh(sv * (k1 + k3 * (sv * sv)))
        at[slot, e * N_KEYS:(e + 1) * N_KEYS, :] = (act * g).reshape(N_KEYS, PEER_TM)
    sbuf[slot] = lax.dot_general(u_ref[...], hf_ref[...], (((1,), (1,)), ((), ())),
                                 preferred_element_type=f32).astype(bf16)

    @pl.when((s > 1) & (last_e == N_ETILES - 1))
    def _():
        o_ref[...] = acc[...].T


def _peer(l, hf, u_all, vt_all, rank2, w2, cnt1, w1):
    k = 0.7978845608028654
    gelu_c = jnp.concatenate([jnp.full((1, PEER_TM), k, f32), jnp.full((1, PEER_TM), k * 0.044715, f32)], 0)
    tile = lambda s, lag: jnp.clip(s - lag, 0, N_PEER_TILES - 1)
    big = pl.BlockSpec((PEER_HEADS, N_KEYS // 16, 16, PEER_TM), lambda s: (0, 0, 0, tile(s, 1) // N_ETILES))
    small = pl.BlockSpec((PEER_HEADS, ROWS_PER_TE, PEER_TM),
                         lambda s: (0, tile(s, 1) % N_ETILES, tile(s, 1) // N_ETILES))
    return pl.pallas_call(
        _peer_kernel,
        out_shape=jax.ShapeDtypeStruct((NT, D), f32),
        grid=(N_PEER_TILES + 2,),
        in_specs=[pl.BlockSpec((PEER_TM, D), lambda s: (tile(s, 0) // N_ETILES, 0)),
                  pl.BlockSpec((None, PEER_TE, D), lambda s: (l, tile(s, 0) % N_ETILES, 0)),
                  pl.BlockSpec((None, D, PEER_TE), lambda s: (l, 0, tile(s, 2) % N_ETILES)),
                  big, big, small, small,
                  pl.BlockSpec((2, PEER_TM), lambda s: (0, 0))],
        out_specs=pl.BlockSpec((PEER_TM, D), lambda s: (tile(s, 2) // N_ETILES, 0)),
        scratch_shapes=[pltpu.VMEM((D, PEER_TM), f32), pltpu.VMEM((2, PEER_TE, PEER_TM), bf16),
                        pltpu.VMEM((2, PEER_TE, PEER_TM), bf16)],
        compiler_params=_cparams(("arbitrary",)), name="peer_experts",
    )(hf, u_all, vt_all, rank2, w2, cnt1, w1, gelu_c)


def _residual_kernel(x_ref, p_ref, mod_ref, o_ref):
    o_ref[...] = x_ref[...] + mod_ref[5:6, :] * p_ref[...]


def _residual(x1, peer_out, mod_l, blk0, nrows):
    row = lambda i: (blk0 + i, 0)
    return pl.pallas_call(
        _residual_kernel,
        out_shape=jax.ShapeDtypeStruct((nrows, D), f32),
        grid=(nrows // TM,),
        in_specs=[pl.BlockSpec((TM, D), row), pl.BlockSpec((TM, D), row),
                  pl.BlockSpec((None, 6, D), lambda i: (_mod_row(blk0 + i, TM), 0, 0))],
        out_specs=pl.BlockSpec((TM, D), lambda i: (i, 0)),
        compiler_params=_cparams(("arbitrary",)), name="residual",
    )(x1, peer_out, mod_l)


def _rope_tables():
    pos = jnp.arange(T_S)
    row = (pos // GRID_W).astype(f32)
    col = (pos % GRID_W).astype(f32)
    n_freq = QK_ROPE // 4
    inv = 1.0 / (ROPE_THETA ** (jnp.arange(n_freq, dtype=f32) / n_freq))
    ang = jnp.concatenate([row[:, None] * inv, col[:, None] * inv], -1)
    cos, sin = jnp.cos(ang), jnp.sin(ang)
    pad = LANES - QK_ROPE
    cos_t = jnp.concatenate([cos, cos, jnp.ones((T_S, pad), f32)], -1)
    sin_t = jnp.concatenate([-sin, sin, jnp.zeros((T_S, pad), f32)], -1)
    cos_t = jnp.concatenate([cos_t, jnp.ones((TM, LANES), f32)], 0)
    sin_t = jnp.concatenate([sin_t, jnp.zeros((TM, LANES), f32)], 0)
    return cos_t, sin_t


def _block_diag(w):
    eye = jnp.eye(LRU_HEADS, dtype=w.dtype)
    return jnp.einsum('hij,hg->higj', w, eye).reshape(LRU_W, LRU_W)


def kernel(x_prompt, x_sample, cache_ckv, cache_krope, state_lru, c, c_ctx, ada_w, ada_b, norm_mix_g, w_in, mla_qa_g, mla_w_qb, mla_q_g, mla_kva_g, mla_w_kvb, mla_k_g, lru_conv_w, lru_conv_b, lru_w_a, lru_b_a, lru_w_i, lru_b_i, lru_lam, cm_dw_w, cm_dw_b, cm_ln_g, cm_ln_b, grp_g, w_out, norm_ffn_g, peer_w_q, peer_keys, peer_u, peer_v):
    cs = jnp.concatenate([c_ctx[None, :], c, jnp.zeros((8 - 1 - NB_S, D), f32)], 0)
    mod = _modulation(cs, ada_w, ada_b).reshape(L, 8, 6, D)
    cos_t, sin_t = _rope_tables()
    x = jnp.concatenate([x_prompt.reshape(NP, D), x_sample.reshape(NS, D)], 0)
    zeros_h0 = jnp.zeros((NB_P, 2, LRU_W), f32)
    u_all = peer_u.astype(bf16)
    vt_all = jnp.swapaxes(peer_v, 1, 2).astype(bf16)

    peer_out, x1 = None, None
    ckv_list, kr_list, h_list = [], [], []
    for l in range(L):
        wi = w_in[l]
        w_in_p = jnp.concatenate([wi[:, :832], jnp.zeros((D, LANES - QK_ROPE), f32), wi[:, 832:]], 1).astype(bf16)
        w_qb_p = jnp.pad(mla_w_qb[l].reshape(Q_LORA, HEADS, QK_HEAD),
                         ((0, 0), (0, 0), (0, HEAD_PAD - QK_HEAD))).reshape(Q_LORA, HEADS * HEAD_PAD).astype(bf16)
        qg_p = jnp.tile(jnp.pad(mla_q_g[l], (0, HEAD_PAD - QK_HEAD)), HEADS)[None, :]
        kgn = mla_k_g[l][None, :QK_NOPE]
        kgr = jnp.pad(mla_k_g[l][QK_NOPE:], (0, LANES - QK_ROPE))[None, :]
        wg = jnp.concatenate([_block_diag(lru_w_a[l, 0]), _block_diag(lru_w_a[l, 1]),
                              _block_diag(lru_w_i[l, 0]), _block_diag(lru_w_i[l, 1])], 1).astype(bf16)
        bg = jnp.concatenate([lru_b_a[l, 0], lru_b_a[l, 1], lru_b_i[l, 0], lru_b_i[l, 1]])[None, :]
        g_att = grp_g[l][None, :1024]
        g_lru = grp_g[l][None, 1024:1536]
        g_conv = grp_g[l][None, 1536:]
        seq_w = [lru_conv_w[l], lru_conv_b[l][None, :], wg, bg, lru_lam[l], g_lru,
                 cm_dw_w[l], cm_dw_b[l][None, :], cm_ln_g[l][None, :], cm_ln_b[l][None, :], g_conv]

        x, (qn, ckv, zkr, zx, zg, zc) = _inproj(
            x1 if l else x, peer_out, mod[l - 1] if l else None, mod[l],
            norm_mix_g[l][None, :], w_in_p, mla_qa_g[l][None, :], mla_kva_g[l][None, :])
        ckv_list.append(ckv[:NP].reshape(NB_P, T_P, KV_LORA))
        kr_list.append(zkr[:NP, :QK_ROPE].reshape(NB_P, T_P, QK_ROPE))

        q = _qproj(qn, w_qb_p, qg_p, cos_t, sin_t)
        ckv_all = jnp.concatenate(
            [jnp.concatenate([cache_ckv[:, l], ckv[NP:].reshape(NB_S, T_S, KV_LORA)], 1).reshape(NB_S * T_KV, KV_LORA),
             ckv[:NP]], 0)
        kr_ctx = jnp.pad(cache_krope[:, l], ((0, 0), (0, 0), (0, LANES - QK_ROPE)))
        kr_all = jnp.concatenate(
            [jnp.concatenate([kr_ctx, zkr[NP:].reshape(NB_S, T_S, LANES)], 1).reshape(NB_S * T_KV, LANES),
             zkr[:NP]], 0)
        k, v = _kvproj(ckv_all, kr_all, mla_w_kvb[l].astype(bf16), kgn, kgr, cos_t, sin_t)

        attn_p = _attention_ctx(q, k, v)
        attn_s = _attention_lat(q, k, v)

        lru_p, conv_p, hfin = _seqmix(zx, zg, zc, zeros_h0, NB_P, T_P, 0, seq_w)
        lru_s, conv_s, _ = _seqmix(zx, zg, zc, state_lru[:, l], NB_S, T_S, NP // T_S, seq_w)
        h_list.append(hfin)

        x1, hf = _outproj(attn_p, attn_s, lru_p, lru_s, conv_p, conv_s, x, mod[l], g_att,
                          w_out[l].astype(bf16), norm_ffn_g[l][None, :])
        st = _peerq(hf, peer_w_q[l].astype(bf16),
                    peer_keys[l].reshape(2 * PEER_HEADS, N_KEYS, N_KEYS).astype(bf16))
        rank2, w2, cnt1, w1 = _topk(st)
        peer_out = _peer(l, hf, u_all, vt_all, rank2, w2, cnt1, w1)

    y_p = _residual(x1, peer_out, mod[L - 1], 0, NP)
    y_s = _residual(x1, peer_out, mod[L - 1], NP // TM, NS)
    new_ckv = jnp.stack(ckv_list, 1)
    new_krope = jnp.stack(kr_list, 1)
    new_lru = jnp.stack(h_list, 1)
    return (y_p.reshape(NB_P, T_P, D), y_s.reshape(NB_S, T_S, D), new_ckv, new_krope, new_lru)
```
